```python
import math
import jax, jax.numpy as jnp
from jax import lax
import numpy as np

D_MODEL = 1024
BATCH = 8
SEQ = 2048
DEPTH = 4

GRID_W = 64
CTX_LEN = 256
N_MIXERS = 2
N_CONV_LAYERS = (DEPTH + N_MIXERS - 1) // N_MIXERS
N_ATTN_LAYERS = DEPTH // N_MIXERS
CONV_WIDTH = 31
QK_HEAD_DIM = 64
V_HEAD_DIM = 2 * QK_HEAD_DIM
ATTN_HEADS = D_MODEL // V_HEAD_DIM
ROT_AXIS_DIM = QK_HEAD_DIM // 2
ROPE_BASE = 10000.0
Q_BLOCK = 128
FFN_HIDDEN = ((-(-8 * D_MODEL // 3) + 255) // 256) * 256
N_MOD = 6
EPS = 1e-6

kernel_name = 'hybrid_conv_diffattn_prefix_dit'


def _rmsnorm(x, g):
    xf = x.astype(jnp.float32)
    y = xf * lax.rsqrt(jnp.mean(xf * xf, axis=-1, keepdims=True) + EPS)
    return (y * g.astype(jnp.float32)).astype(x.dtype)


def _layernorm(x, g, b):
    xf = x.astype(jnp.float32)
    mu = jnp.mean(xf, axis=-1, keepdims=True)
    var = jnp.mean(jnp.square(xf - mu), axis=-1, keepdims=True)
    y = (xf - mu) * lax.rsqrt(var + EPS)
    return (y * g.astype(jnp.float32) + b.astype(jnp.float32)).astype(x.dtype)


def _modulate(h, shift, scale):
    return h * (1 + scale) + shift


def _axial_rope_tables(rows, dtype):
    row = jnp.repeat(jnp.arange(rows, dtype=jnp.float32), GRID_W)
    col = jnp.tile(jnp.arange(GRID_W, dtype=jnp.float32), rows)
    inv = ROPE_BASE ** (-2.0 * jnp.arange(ROT_AXIS_DIM // 2, dtype=jnp.float32) / ROT_AXIS_DIM)
    ang_r = row[:, None] * inv[None, :]
    ang_c = col[:, None] * inv[None, :]
    ang = jnp.concatenate([ang_r, ang_r, ang_c, ang_c], axis=-1)
    return jnp.cos(ang).astype(dtype), jnp.sin(ang).astype(dtype)


def _apply_rope(x, cos, sin):
    xs = x.reshape(x.shape[:-1] + (2, 2, ROT_AXIS_DIM // 2))
    rot = jnp.stack([-xs[..., 1, :], xs[..., 0, :]], axis=-2).reshape(x.shape)
    c = cos[None, :, None, None, :]
    s = sin[None, :, None, None, :]
    return x * c + rot * s


def _conv_module(h, pw1_w, pw1_b, dw_w, dw_b, ln_g, ln_b, pw2_w, pw2_b):
    u = h @ pw1_w + pw1_b
    a, g = jnp.split(u, 2, axis=-1)
    u = a * jax.nn.sigmoid(g)
    u = lax.conv_general_dilated(
        u, dw_w[:, None, :].astype(u.dtype), window_strides=(1,),
        padding=[(CONV_WIDTH // 2, CONV_WIDTH // 2)],
        dimension_numbers=('NWC', 'WIO', 'NWC'),
        feature_group_count=D_MODEL) + dw_b
    u = jax.nn.silu(_layernorm(u, ln_g, ln_b))
    return u @ pw2_w + pw2_b


def _qkv(h, w_qkv):
    B, L, _ = h.shape
    q, k, v = jnp.split(h @ w_qkv, 3, axis=-1)
    q = q.reshape(B, L, ATTN_HEADS, 2, QK_HEAD_DIM)
    k = k.reshape(B, L, ATTN_HEADS, 2, QK_HEAD_DIM)
    v = v.reshape(B, L, ATTN_HEADS, V_HEAD_DIM)
    return q, k, v


def _diff_core(q, k, v, lam):
    s = jnp.einsum('bqhrd,bkhrd->bhrqk', q, k, preferred_element_type=jnp.float32)
    p = jax.nn.softmax(s * (1.0 / math.sqrt(QK_HEAD_DIM)), axis=-1)
    a = p[:, :, 0] - lam * p[:, :, 1]
    return jnp.einsum('bhqk,bkhe->bqhe', a.astype(v.dtype), v)


def _diff_attention(n_lat, n_ctx, w_qkv, lam_vec, subln_g, w_o, lam_init, cos, sin, with_ctx):
    B, L, _ = n_lat.shape
    q_l, k_l, v_l = _qkv(n_lat, w_qkv)
    q_c, k_c, v_c = _qkv(n_ctx, w_qkv)
    q_l = _apply_rope(q_l, cos, sin)
    k_l = _apply_rope(k_l, cos, sin)
    lv = lam_vec.astype(jnp.float32)
    lam = jnp.exp(jnp.dot(lv[0], lv[1])) - jnp.exp(jnp.dot(lv[2], lv[3])) + lam_init
    k_all = jnp.concatenate([k_c, k_l], axis=1)
    v_all = jnp.concatenate([v_c, v_l], axis=1)
    nb = L // Q_BLOCK
    qb = jnp.swapaxes(q_l.reshape(B, nb, Q_BLOCK, ATTN_HEADS, 2, QK_HEAD_DIM), 0, 1)
    ob = lax.map(lambda qblk: _diff_core(qblk, k_all, v_all, lam), qb)
    o_l = jnp.swapaxes(ob, 0, 1).reshape(B, L, ATTN_HEADS, V_HEAD_DIM)
    o_l = (_rmsnorm(o_l, subln_g) * (1.0 - lam_init)).reshape(B, L, D_MODEL) @ w_o
    if not with_ctx:
        return o_l, None
    o_c = _diff_core(q_c, k_c, v_c, lam)
    o_c = (_rmsnorm(o_c, subln_g) * (1.0 - lam_init)).reshape(B, n_ctx.shape[1], D_MODEL) @ w_o
    return o_l, o_c


def _swiglu(h, w_in, w_out):
    g, u = jnp.split(h @ w_in, 2, axis=-1)
    return (jax.nn.silu(g) * u) @ w_out


def setup_inputs(seed: int = 0) -> dict:
    key = jax.random.key(seed)
    ks = jax.random.split(key, 22)
    D = D_MODEL

    def nrm(k, shape, scale):
        return jax.random.normal(k, shape, jnp.float32) * scale

    return {
        'x': nrm(ks[0], (BATCH, SEQ, D), 1.0),
        'c': nrm(ks[1], (BATCH, D), 1.0),
        'ctx': nrm(ks[2], (BATCH, CTX_LEN, D), 1.0),
        'c_ctx': nrm(ks[3], (D,), 1.0),
        'mod_w': nrm(ks[4], (DEPTH, D, N_MOD * D), 0.5 * D ** -0.5),
        'mod_b': nrm(ks[5], (DEPTH, N_MOD * D), 0.02),
        'norm_g': 1.0 + nrm(ks[6], (DEPTH, 2, D), 0.02),
        'conv_pw1_w': nrm(ks[7], (N_CONV_LAYERS, D, 2 * D), D ** -0.5),
        'conv_pw1_b': nrm(ks[8], (N_CONV_LAYERS, 2 * D), 0.02),
        'conv_dw_w': nrm(ks[9], (N_CONV_LAYERS, CONV_WIDTH, D), CONV_WIDTH ** -0.5),
        'conv_dw_b': nrm(ks[10], (N_CONV_LAYERS, D), 0.02),
        'conv_ln_g': 1.0 + nrm(ks[11], (N_CONV_LAYERS, D), 0.02),
        'conv_ln_b': nrm(ks[12], (N_CONV_LAYERS, D), 0.02),
        'conv_pw2_w': nrm(ks[13], (N_CONV_LAYERS, D, D), D ** -0.5),
        'conv_pw2_b': nrm(ks[14], (N_CONV_LAYERS, D), 0.02),
        'attn_w_qkv': nrm(ks[15], (N_ATTN_LAYERS, D, 3 * D), D ** -0.5),
        'attn_lambda': nrm(ks[16], (N_ATTN_LAYERS, 4, QK_HEAD_DIM), 0.1),
        'attn_subln_g': 1.0 + nrm(ks[17], (N_ATTN_LAYERS, V_HEAD_DIM), 0.02),
        'attn_w_o': nrm(ks[18], (N_ATTN_LAYERS, D, D), D ** -0.5),
        'ffn_w_in': nrm(ks[19], (DEPTH, D, 2 * FFN_HIDDEN), D ** -0.5),
        'ffn_w_out': nrm(ks[20], (DEPTH, FFN_HIDDEN, D), FFN_HIDDEN ** -0.5),
        'final_g': 1.0 + nrm(ks[21], (D,), 0.02),
    }


def reference(x, c, ctx, c_ctx, mod_w, mod_b, norm_g,
              conv_pw1_w, conv_pw1_b, conv_dw_w, conv_dw_b, conv_ln_g, conv_ln_b, conv_pw2_w, conv_pw2_b,
              attn_w_qkv, attn_lambda, attn_subln_g, attn_w_o,
              ffn_w_in, ffn_w_out, final_g):
    L = x.shape[1]
    ROWS = L // GRID_W
    cos, sin = _axial_rope_tables(ROWS, x.dtype)
    s_lat = jax.nn.silu(c)
    s_ctx = jax.nn.silu(c_ctx)
    h_lat, h_ctx = x, ctx
    for i in range(DEPTH):
        with_ctx = i < DEPTH - 1
        m_l = jnp.split((s_lat @ mod_w[i] + mod_b[i])[:, None, :], N_MOD, axis=-1)
        m_c = jnp.split((s_ctx @ mod_w[i] + mod_b[i])[None, None, :], N_MOD, axis=-1)
        n_l = _modulate(_rmsnorm(h_lat, norm_g[i, 0]), m_l[0], m_l[1])
        n_c = _modulate(_rmsnorm(h_ctx, norm_g[i, 0]), m_c[0], m_c[1])
        j = i // N_MIXERS
        if i % N_MIXERS == 0:
            cp = (conv_pw1_w[j], conv_pw1_b[j], conv_dw_w[j], conv_dw_b[j],
                  conv_ln_g[j], conv_ln_b[j], conv_pw2_w[j], conv_pw2_b[j])
            y_l = _conv_module(n_l, *cp)
            y_c = _conv_module(n_c, *cp) if with_ctx else None
        else:
            lam_init = 0.8 - 0.6 * math.exp(-0.3 * i)
            y_l, y_c = _diff_attention(n_l, n_c, attn_w_qkv[j], attn_lambda[j], attn_subln_g[j],
                                       attn_w_o[j], lam_init, cos, sin, with_ctx)
        h_lat = h_lat + m_l[2] * y_l
        f_l = _modulate(_rmsnorm(h_lat, norm_g[i, 1]), m_l[3], m_l[4])
        h_lat = h_lat + m_l[5] * _swiglu(f_l, ffn_w_in[i], ffn_w_out[i])
        if with_ctx:
            h_ctx = h_ctx + m_c[2] * y_c
            f_c = _modulate(_rmsnorm(h_ctx, norm_g[i, 1]), m_c[3], m_c[4])
            h_ctx = h_ctx + m_c[5] * _swiglu(f_c, ffn_w_in[i], ffn_w_out[i])
    return _rmsnorm(h_lat, final_g)
```

```python
import functools
import math

import jax
import jax.numpy as jnp
from jax import lax
from jax.experimental import pallas as pl
from jax.experimental.pallas import tpu as pltpu

D = 1024
B = 8
L = 2048
LC = 256
DEPTH = 4
GRID_W = 64
N_MIXERS = 2
CONV_W = 31
CONV_HALF = CONV_W // 2
DH = 64
DV = 2 * DH
H = D // DV
ROT = DH // 2
ROPE_BASE = 10000.0
FH = 2816
N_MOD = 6
EPS = 1e-6

T_LAT = B * L
T_CTX = B * LC
T_ALL = T_LAT + T_CTX
MOD_ROWS = 16
CTX_MOD_ROW = B

TM = 512
TL = 256
HALO = 16
TQ = 512
TF = 256
VMEM_LIMIT = 56 * 1024 * 1024

bf16 = jnp.bfloat16
f32 = jnp.float32


def _dot(a, b):
    return jnp.dot(a, b, preferred_element_type=f32)


def _dot_nt(a, b):
    return lax.dot_general(a, b, (((1,), (1,)), ((), ())), preferred_element_type=f32)


def _params(*sem):
    return pltpu.CompilerParams(dimension_semantics=sem, vmem_limit_bytes=VMEM_LIMIT)


def _resident(shape):
    nd = len(shape)
    return pl.BlockSpec(shape, lambda *_: (0,) * nd, pipeline_mode=pl.Buffered(1))


def _mod_row(tile_rows):
    return lambda i: jnp.minimum((i * tile_rows) // L, CTX_MOD_ROW)


def _norm_mod(x, g, shift, scale):
    ms = jnp.mean(x * x, axis=-1, keepdims=True)
    y = x * lax.rsqrt(ms + EPS) * g
    return y * (1.0 + scale) + shift


MOD_TN = 1536


def _mod_kernel(c_ref, w_ref, b_ref, o_ref):
    c = c_ref[...]
    s = (c * jax.nn.sigmoid(c)).astype(bf16)
    o_ref[...] = _dot(s, w_ref[...].astype(bf16)) + b_ref[...]


def _modulation(cvec, mod_w, mod_b):
    n = N_MOD * D
    return pl.pallas_call(
        _mod_kernel,
        grid=(DEPTH, n // MOD_TN),
        in_specs=[
            pl.BlockSpec((MOD_ROWS, D), lambda i, j: (0, 0)),
            pl.BlockSpec((None, D, MOD_TN), lambda i, j: (i, 0, j)),
            pl.BlockSpec((None, 1, MOD_TN), lambda i, j: (i, 0, j)),
        ],
        out_specs=pl.BlockSpec((None, MOD_ROWS, MOD_TN), lambda i, j: (i, 0, j)),
        out_shape=jax.ShapeDtypeStruct((DEPTH, MOD_ROWS, n), f32),
        compiler_params=_params("arbitrary", "arbitrary"),
        name="modulation",
    )(cvec, mod_w, mod_b.reshape(DEPTH, 1, n))


def _ffn_kernel(h_ref, mod_ref, g_ref, win_ref, wout_ref, o_ref, xn_scr, a_scr):
    x = h_ref[...]
    xn_scr[...] = _norm_mod(x, g_ref[...], mod_ref[3:4, :], mod_ref[4:5, :]).astype(bf16)
    for c in range(FH // TF):
        gate = _dot(xn_scr[...], win_ref[:, c * TF:(c + 1) * TF])
        up = _dot(xn_scr[...], win_ref[:, FH + c * TF:FH + (c + 1) * TF])
        a_scr[:, c * TF:(c + 1) * TF] = (gate * jax.nn.sigmoid(gate) * up).astype(bf16)
    y = _dot(a_scr[...], wout_ref[...])
    o_ref[...] = x + mod_ref[5:6, :] * y


def _ffn(h, mods, norm_g, w_in, w_out, layer, n_tok):
    return pl.pallas_call(
        _ffn_kernel,
        grid=(n_tok // TM,),
        in_specs=[
            pl.BlockSpec((TM, D), lambda i: (i, 0)),
            pl.BlockSpec((None, None, N_MOD, D), lambda i: (layer, _mod_row(TM)(i), 0, 0)),
            pl.BlockSpec((None, None, 1, D), lambda i: (layer, 1, 0, 0)),
            _resident((D, 2 * FH)),
            _resident((FH, D)),
        ],
        out_specs=pl.BlockSpec((TM, D), lambda i: (i, 0)),
        out_shape=jax.ShapeDtypeStruct((n_tok, D), f32),
        scratch_shapes=[pltpu.VMEM((TM, D), bf16), pltpu.VMEM((TM, FH), bf16)],
        compiler_params=_params("arbitrary"),
        name=f"ffn{layer}",
    )(h, mods, norm_g, w_in, w_out)


PW1_TN = 256


def _pw1_kernel(h_ref, mod_ref, g_ref, w_ref, b_ref, o_ref, xn_scr):
    xn_scr[...] = _norm_mod(h_ref[...], g_ref[...], mod_ref[0:1, :], mod_ref[1:2, :]).astype(bf16)
    for c in range(D // PW1_TN):
        lo, hi = c * PW1_TN, (c + 1) * PW1_TN
        a = _dot(xn_scr[...], w_ref[:, lo:hi]) + b_ref[:, lo:hi]
        g = _dot(xn_scr[...], w_ref[:, D + lo:D + hi]) + b_ref[:, D + lo:D + hi]
        o_ref[:, lo:hi] = a * jax.nn.sigmoid(g)


def _pw1(h, mods, norm_g, w, b, layer, n_tok):
    return pl.pallas_call(
        _pw1_kernel,
        grid=(n_tok // TM,),
        in_specs=[
            pl.BlockSpec((TM, D), lambda i: (i, 0)),
            pl.BlockSpec((None, None, N_MOD, D), lambda i: (layer, _mod_row(TM)(i), 0, 0)),
            pl.BlockSpec((None, None, 1, D), lambda i: (layer, 0, 0, 0)),
            _resident((D, 2 * D)),
            _resident((1, 2 * D)),
        ],
        out_specs=pl.BlockSpec((TM, D), lambda i: (i, 0)),
        out_shape=jax.ShapeDtypeStruct((n_tok, D), f32),
        scratch_shapes=[pltpu.VMEM((TM, D), bf16)],
        compiler_params=_params("arbitrary"),
        name=f"conv_pw1_{layer}",
    )(h, mods, norm_g, w, b)


CONV_RC = 32
LANES = 128


def _conv2_kernel(u_ref, up_ref, un_ref, h_ref, mod_ref, dww_ref, dwb_ref, lng_ref, lnb_ref,
                  w2_ref, b2_ref, o_ref, pad_scr, cv_scr):
    t = pl.program_id(0)
    tiles_per_seq = L // TL
    is_ctx = t >= T_LAT // TL
    first = jnp.logical_or(is_ctx, t % tiles_per_seq == 0)
    last = jnp.logical_or(is_ctx, t % tiles_per_seq == tiles_per_seq - 1)
    pad_scr[0:HALO, :] = jnp.where(first, 0.0, up_ref[...])
    pad_scr[HALO:HALO + TL, :] = u_ref[...]
    pad_scr[HALO + TL:HALO + TL + HALO, :] = jnp.where(last, 0.0, un_ref[...])

    base = HALO - CONV_HALF
    for cb in range(D // LANES):
        lanes = slice(cb * LANES, (cb + 1) * LANES)
        wcol = dww_ref[:, lanes]
        for rb in range(TL // CONV_RC):
            r0 = rb * CONV_RC
            acc = jnp.zeros((CONV_RC, LANES), f32)
            for k in range(CONV_W):
                acc = acc + wcol[k:k + 1, :] * pad_scr[r0 + base + k:r0 + base + k + CONV_RC, lanes]
            cv_scr[r0:r0 + CONV_RC, lanes] = acc

    v = cv_scr[...] + dwb_ref[...]
    mu = jnp.mean(v, axis=-1, keepdims=True)
    vc = v - mu
    var = jnp.mean(vc * vc, axis=-1, keepdims=True)
    y = vc * lax.rsqrt(var + EPS) * lng_ref[...] + lnb_ref[...]
    y = (y * jax.nn.sigmoid(y)).astype(bf16)
    z = _dot(y, w2_ref[...]) + b2_ref[...]
    o_ref[...] = h_ref[...] + mod_ref[2:3, :] * z


def _conv2(u, h, mods, dw_w, dw_b, ln_g, ln_b, w2, b2, layer, n_tok):
    nblk = n_tok // HALO
    per = TL // HALO
    return pl.pallas_call(
        _conv2_kernel,
        grid=(n_tok // TL,),
        in_specs=[
            pl.BlockSpec((TL, D), lambda i: (i, 0)),
            pl.BlockSpec((HALO, D), lambda i: (jnp.maximum(i * per - 1, 0), 0)),
            pl.BlockSpec((HALO, D), lambda i: (jnp.minimum((i + 1) * per, nblk - 1), 0)),
            pl.BlockSpec((TL, D), lambda i: (i, 0)),
            pl.BlockSpec((None, None, N_MOD, D), lambda i: (layer, _mod_row(TL)(i), 0, 0)),
            _resident((CONV_W, D)),
            _resident((1, D)),
            _resident((1, D)),
            _resident((1, D)),
            _resident((D, D)),
            _resident((1, D)),
        ],
        out_specs=pl.BlockSpec((TL, D), lambda i: (i, 0)),
        out_shape=jax.ShapeDtypeStruct((n_tok, D), f32),
        scratch_shapes=[pltpu.VMEM((TL + 2 * HALO, D), f32), pltpu.VMEM((TL, D), f32)],
        compiler_params=_params("arbitrary"),
        name=f"conv_dw_pw2_{layer}",
    )(u, u, u, h, mods, dw_w, dw_b, ln_g, ln_b, w2, b2)


def _rope(x, cos, sin_signed, first_half):
    nxt = pltpu.roll(x, LANES - ROT // 2, axis=1)
    prv = pltpu.roll(x, ROT // 2, axis=1)
    return x * cos + jnp.where(first_half, nxt, prv) * sin_signed


def _qkv_kernel(h_ref, mod_ref, g_ref, w_ref, cos_ref, sin_ref, q_ref, k_ref, v_ref, xn_scr):
    xn_scr[...] = _norm_mod(h_ref[...], g_ref[...], mod_ref[0:1, :], mod_ref[1:2, :]).astype(bf16)
    cos = cos_ref[...]
    sin = sin_ref[...]
    lane = lax.broadcasted_iota(jnp.int32, (TM, LANES), 1)
    first_half = (lane % ROT) < (ROT // 2)
    scale = 1.0 / math.sqrt(DH)
    for hd in range(H):
        lanes = slice(hd * DV, (hd + 1) * DV)
        q = _dot(xn_scr[...], w_ref[:, hd * DV:(hd + 1) * DV])
        q_ref[:, lanes] = (_rope(q, cos, sin, first_half) * scale).astype(bf16)
        k = _dot(xn_scr[...], w_ref[:, D + hd * DV:D + (hd + 1) * DV])
        k_ref[:, lanes] = _rope(k, cos, sin, first_half).astype(bf16)
    v_ref[...] = _dot(xn_scr[...], w_ref[:, 2 * D:3 * D]).astype(bf16)


def _qkv(h, mods, norm_g, w, cos_tab, sin_tab, layer):
    lat_tiles = T_LAT // TM
    per_seq = L // TM
    tab_idx = lambda i: (jnp.where(i < lat_tiles, i % per_seq, per_seq), 0)
    out = jax.ShapeDtypeStruct((T_ALL, D), bf16)
    return pl.pallas_call(
        _qkv_kernel,
        grid=(T_ALL // TM,),
        in_specs=[
            pl.BlockSpec((TM, D), lambda i: (i, 0)),
            pl.BlockSpec((None, None, N_MOD, D), lambda i: (layer, _mod_row(TM)(i), 0, 0)),
            pl.BlockSpec((None, None, 1, D), lambda i: (layer, 0, 0, 0)),
            _resident((D, 3 * D)),
            pl.BlockSpec((TM, LANES), tab_idx),
            pl.BlockSpec((TM, LANES), tab_idx),
        ],
        out_specs=[pl.BlockSpec((TM, D), lambda i: (i, 0))] * 3,
        out_shape=[out, out, out],
        scratch_shapes=[pltpu.VMEM((TM, D), bf16)],
        compiler_params=_params("arbitrary"),
        name=f"attn_qkv_{layer}",
    )(h, mods, norm_g, w, cos_tab, sin_tab)


def _rope_tables():
    t = jnp.arange(L, dtype=f32)
    row = jnp.floor(t / GRID_W)
    col = t - row * GRID_W
    inv = ROPE_BASE ** (-2.0 * jnp.arange(ROT // 2, dtype=f32) / ROT)
    ang_r = row[:, None] * inv[None, :]
    ang_c = col[:, None] * inv[None, :]
    ang = jnp.concatenate([ang_r, ang_r, ang_c, ang_c], axis=-1)
    cos = jnp.cos(ang)
    sin = jnp.sin(ang)
    sign = jnp.where((jnp.arange(DH) % ROT) < ROT // 2, -1.0, 1.0).astype(f32)
    sin = sin * sign[None, :]
    cos = jnp.concatenate([jnp.tile(cos, (1, 2)), jnp.ones((TM, LANES), f32)], axis=0)
    sin = jnp.concatenate([jnp.tile(sin, (1, 2)), jnp.zeros((TM, LANES), f32)], axis=0)
    return cos, sin


def _attn_body(q_ref, k_refs, v_refs, lam_ref, g_ref, o_ref, lam_init, tq):
    q = q_ref[...]
    lane = lax.broadcasted_iota(jnp.int32, (tq, LANES), 1)
    zero = jnp.zeros_like(q)
    qq = jnp.concatenate([jnp.where(lane < DH, q, zero), jnp.where(lane >= DH, q, zero)], axis=0)
    s = [_dot_nt(qq, k_ref[...]) for k_ref in k_refs]
    m = functools.reduce(jnp.maximum, [jnp.max(x, axis=-1, keepdims=True) for x in s])
    p = [jnp.exp(x - m) for x in s]
    den = functools.reduce(jnp.add, [jnp.sum(x, axis=-1, keepdims=True) for x in p])
    inv = 1.0 / den
    lv = lam_ref[...]
    lam = (jnp.exp(jnp.sum(lv[0:1, :] * lv[1:2, :], axis=-1, keepdims=True))
           - jnp.exp(jnp.sum(lv[2:3, :] * lv[3:4, :], axis=-1, keepdims=True)) + lam_init)
    w1 = inv[:tq]
    w2 = inv[tq:] * lam
    o = None
    for x, v_ref in zip(p, v_refs):
        a = (x[:tq] * w1 - x[tq:] * w2).astype(bf16)
        part = _dot(a, v_ref[...])
        o = part if o is None else o + part
    ms = jnp.mean(o * o, axis=-1, keepdims=True)
    o = o * lax.rsqrt(ms + EPS) * g_ref[...] * (1.0 - lam_init)
    o_ref[...] = o.astype(bf16)


def _attn_lat_kernel(q_ref, kc_ref, kl_ref, vc_ref, vl_ref, lam_ref, g_ref, o_ref, *, lam_init):
    _attn_body(q_ref, (kc_ref, kl_ref), (vc_ref, vl_ref), lam_ref, g_ref, o_ref, lam_init, TQ)


def _attn_ctx_kernel(o_in_ref, q_ref, kc_ref, vc_ref, lam_ref, g_ref, o_ref, *, lam_init):
    del o_in_ref
    _attn_body(q_ref, (kc_ref,), (vc_ref,), lam_ref, g_ref, o_ref, lam_init, LC)


def _attention(q, k, v, lam_vec, subln_g, lam_init, with_ctx):
    nq = L // TQ
    ctx0 = T_LAT // LC
    q_spec = pl.BlockSpec((TQ, DV), lambda b, h, i: (b * nq + i, h))
    ctx_spec = pl.BlockSpec((LC, DV), lambda b, h, i: (ctx0 + b, h))
    lat_spec = pl.BlockSpec((L, DV), lambda b, h, i: (b, h))
    n_out = T_ALL if with_ctx else T_LAT
    o = pl.pallas_call(
        functools.partial(_attn_lat_kernel, lam_init=lam_init),
        grid=(B, H, nq),
        in_specs=[q_spec, ctx_spec, lat_spec, ctx_spec, lat_spec,
                  pl.BlockSpec((4, DH), lambda b, h, i: (0, 0)),
                  pl.BlockSpec((1, DV), lambda b, h, i: (0, 0))],
        out_specs=q_spec,
        out_shape=jax.ShapeDtypeStruct((n_out, D), bf16),
        compiler_params=_params("arbitrary", "arbitrary", "arbitrary"),
        name="diff_attn_latent",
    )(q, k, k, v, v, lam_vec, subln_g)
    if not with_ctx:
        return o
    cspec = pl.BlockSpec((LC, DV), lambda b, h: (ctx0 + b, h))
    return pl.pallas_call(
        functools.partial(_attn_ctx_kernel, lam_init=lam_init),
        grid=(B, H),
        in_specs=[pl.BlockSpec(memory_space=pl.ANY), cspec, cspec, cspec,
                  pl.BlockSpec((4, DH), lambda b, h: (0, 0)),
                  pl.BlockSpec((1, DV), lambda b, h: (0, 0))],
        out_specs=cspec,
        out_shape=jax.ShapeDtypeStruct((T_ALL, D), bf16),
        input_output_aliases={0: 0},
        compiler_params=_params("arbitrary", "arbitrary"),
        name="diff_attn_context",
    )(o, q, k, v, lam_vec, subln_g)


def _wo_kernel(o_ref, h_ref, mod_ref, w_ref, out_ref):
    out_ref[...] = h_ref[...] + mod_ref[2:3, :] * _dot(o_ref[...], w_ref[...])


def _wo(o, h, mods, w, layer, n_tok):
    return pl.pallas_call(
        _wo_kernel,
        grid=(n_tok // TM,),
        in_specs=[
            pl.BlockSpec((TM, D), lambda i: (i, 0)),
            pl.BlockSpec((TM, D), lambda i: (i, 0)),
            pl.BlockSpec((None, None, N_MOD, D), lambda i: (layer, _mod_row(TM)(i), 0, 0)),
            _resident((D, D)),
        ],
        out_specs=pl.BlockSpec((TM, D), lambda i: (i, 0)),
        out_shape=jax.ShapeDtypeStruct((n_tok, D), f32),
        compiler_params=_params("arbitrary"),
        name=f"attn_wo_{layer}",
    )(o, h, mods, w)


def _final_kernel(h_ref, g_ref, o_ref):
    x = h_ref[...]
    ms = jnp.mean(x * x, axis=-1, keepdims=True)
    o_ref[...] = x * lax.rsqrt(ms + EPS) * g_ref[...]


def _final_norm(h, g):
    return pl.pallas_call(
        _final_kernel,
        grid=(T_LAT // TM,),
        in_specs=[pl.BlockSpec((TM, D), lambda i: (i, 0)), pl.BlockSpec((1, D), lambda i: (0, 0))],
        out_specs=pl.BlockSpec((TM, D), lambda i: (i, 0)),
        out_shape=jax.ShapeDtypeStruct((T_LAT, D), f32),
        compiler_params=_params("arbitrary"),
        name="final_norm",
    )(h, g)


def kernel(x, c, ctx, c_ctx, mod_w, mod_b, norm_g, conv_pw1_w, conv_pw1_b, conv_dw_w, conv_dw_b, conv_ln_g, conv_ln_b, conv_pw2_w, conv_pw2_b, attn_w_qkv, attn_lambda, attn_subln_g, attn_w_o, ffn_w_in, ffn_w_out, final_g):
    assert x.shape == (B, L, D) and ctx.shape == (B, LC, D)
    h = jnp.concatenate([x.reshape(T_LAT, D), ctx.reshape(T_CTX, D)], axis=0)
    cvec = jnp.concatenate([c, c_ctx[None, :], jnp.zeros((MOD_ROWS - B - 1, D), f32)], axis=0)
    mods = _modulation(cvec, mod_w, mod_b).reshape(DEPTH, MOD_ROWS, N_MOD, D)
    norm_g4 = norm_g.reshape(DEPTH, 2, 1, D)
    cos_tab, sin_tab = _rope_tables()

    for i in range(DEPTH):
        with_ctx = i < DEPTH - 1
        n_tok = T_ALL if with_ctx else T_LAT
        j = i // N_MIXERS
        if i % N_MIXERS == 0:
            u = _pw1(h, mods, norm_g4, conv_pw1_w[j].astype(bf16), conv_pw1_b[j].reshape(1, 2 * D), i, n_tok)
            h = _conv2(u, h, mods, conv_dw_w[j], conv_dw_b[j].reshape(1, D), conv_ln_g[j].reshape(1, D),
                       conv_ln_b[j].reshape(1, D), conv_pw2_w[j].astype(bf16), conv_pw2_b[j].reshape(1, D),
                       i, n_tok)
        else:
            lam_init = 0.8 - 0.6 * math.exp(-0.3 * i)
            q, k, v = _qkv(h, mods, norm_g4, attn_w_qkv[j].astype(bf16), cos_tab, sin_tab, i)
            o = _attention(q, k, v, attn_lambda[j], attn_subln_g[j].reshape(1, DV), lam_init, with_ctx)
            h = _wo(o, h, mods, attn_w_o[j].astype(bf16), i, n_tok)
        h = _ffn(h, mods, norm_g4, ffn_w_in[i].astype(bf16), ffn_w_out[i].astype(bf16), i, n_tok)
    return _final_norm(h, final_g.reshape(1, D)).reshape(B, L, D)
```

```python
import functools
import math

import jax
import jax.numpy as jnp
from jax import lax
from jax.experimental import pallas as pl
from jax.experimental.pallas import tpu as pltpu

D = 1024
B = 8
L = 2048
LC = 256
DEPTH = 4
GRID_W = 64
N_MIXERS = 2
CONV_W = 31
CONV_HALF = CONV_W // 2
DH = 64
DV = 2 * DH
H = D // DV
ROT = DH // 2
ROPE_BASE = 10000.0
FH = 2816
N_MOD = 6
EPS = 1e-6

LANES = 128
CG = D // LANES

T_LAT = B * L
T_CTX = B * LC
T_ALL = T_LAT + T_CTX
MOD_ROWS = 16
CTX_MOD_ROW = B

TM = 512
TL = 256
HALO = 16
TQ = 2048
TQ_SUB = 256
Q_SCALE = math.log2(math.e) / math.sqrt(DH)
TF = 256
VMEM_LIMIT = 56 * 1024 * 1024

bf16 = jnp.bfloat16
f32 = jnp.float32


def _dot(a, b):
    return jnp.dot(a, b, preferred_element_type=f32)


def _dot_nt(a, b):
    return lax.dot_general(a, b, (((1,), (1,)), ((), ())), preferred_element_type=f32)


def _params(*sem):
    return pltpu.CompilerParams(dimension_semantics=sem, vmem_limit_bytes=VMEM_LIMIT)


def _resident(shape):
    nd = len(shape)
    return pl.BlockSpec(shape, lambda *_: (0,) * nd, pipeline_mode=pl.Buffered(1))


def _mod_row(tile_rows):
    return lambda i: jnp.minimum((i * tile_rows) // L, CTX_MOD_ROW)


def _norm_mod(x, g, shift, scale):
    ms = jnp.mean(x * x, axis=-1, keepdims=True)
    y = x * lax.rsqrt(ms + EPS) * g
    return y * (1.0 + scale) + shift


MOD_TN = 1536


def _mod_kernel(c_ref, w_ref, b_ref, o_ref):
    c = c_ref[...]
    s = (c * jax.nn.sigmoid(c)).astype(bf16)
    o_ref[...] = _dot(s, w_ref[...].astype(bf16)) + b_ref[...]


def _modulation(cvec, mod_w, mod_b):
    n = N_MOD * D
    return pl.pallas_call(
        _mod_kernel,
        grid=(DEPTH, n // MOD_TN),
        in_specs=[
            pl.BlockSpec((MOD_ROWS, D), lambda i, j: (0, 0)),
            pl.BlockSpec((None, D, MOD_TN), lambda i, j: (i, 0, j)),
            pl.BlockSpec((None, 1, MOD_TN), lambda i, j: (i, 0, j)),
        ],
        out_specs=pl.BlockSpec((None, MOD_ROWS, MOD_TN), lambda i, j: (i, 0, j)),
        out_shape=jax.ShapeDtypeStruct((DEPTH, MOD_ROWS, n), f32),
        compiler_params=_params("arbitrary", "arbitrary"),
        name="modulation",
    )(cvec, mod_w, mod_b.reshape(DEPTH, 1, n))


def _ffn_kernel(*refs, attn_in, final_out):
    h_ref, mod_ref, g_ref, win_ref, wout_ref = refs[:5]
    rest = list(refs[5:])
    ao_ref, wo_ref = (rest.pop(0), rest.pop(0)) if attn_in else (None, None)
    fg_ref = rest.pop(0) if final_out else None
    o_ref, xn_scr, a_scr = rest

    x = h_ref[...]
    if attn_in:
        x = x + mod_ref[2:3, :] * _dot(ao_ref[...], wo_ref[...])
    xn_scr[...] = _norm_mod(x, g_ref[...], mod_ref[3:4, :], mod_ref[4:5, :]).astype(bf16)
    for c in range(FH // TF):
        gate = _dot(xn_scr[...], win_ref[:, c * TF:(c + 1) * TF])
        up = _dot(xn_scr[...], win_ref[:, FH + c * TF:FH + (c + 1) * TF])
        a_scr[:, c * TF:(c + 1) * TF] = (gate * jax.nn.sigmoid(gate) * up).astype(bf16)
    y = x + mod_ref[5:6, :] * _dot(a_scr[...], wout_ref[...])
    if final_out:
        ms = jnp.mean(y * y, axis=-1, keepdims=True)
        y = y * lax.rsqrt(ms + EPS) * fg_ref[...]
    o_ref[...] = y


def _ffn(h, mods, norm_g, w_in, w_out, layer, n_tok, attn=None, final_g=None):
    row = lambda i: (i, 0)
    in_specs = [
        pl.BlockSpec((TM, D), row),
        pl.BlockSpec((None, None, N_MOD, D), lambda i: (layer, _mod_row(TM)(i), 0, 0)),
        pl.BlockSpec((None, None, 1, D), lambda i: (layer, 1, 0, 0)),
        _resident((D, 2 * FH)),
        _resident((FH, D)),
    ]
    args = [h, mods, norm_g, w_in, w_out]
    if attn is not None:
        in_specs += [pl.BlockSpec((TM, D), row), _resident((D, D))]
        args += list(attn)
    if final_g is not None:
        in_specs.append(_resident((1, D)))
        args.append(final_g)
    return pl.pallas_call(
        functools.partial(_ffn_kernel, attn_in=attn is not None, final_out=final_g is not None),
        grid=(n_tok // TM,),
        in_specs=in_specs,
        out_specs=pl.BlockSpec((TM, D), row),
        out_shape=jax.ShapeDtypeStruct((n_tok, D), f32),
        scratch_shapes=[pltpu.VMEM((TM, D), bf16), pltpu.VMEM((TM, FH), bf16)],
        compiler_params=_params("arbitrary"),
        name=f"ffn{layer}",
    )(*args)


PW1_TN = 256


def _pw1_kernel(h_ref, mod_ref, g_ref, w_ref, b_ref, o_ref, xn_scr):
    xn_scr[...] = _norm_mod(h_ref[...], g_ref[...], mod_ref[0:1, :], mod_ref[1:2, :]).astype(bf16)
    for c in range(D // PW1_TN):
        lo, hi = c * PW1_TN, (c + 1) * PW1_TN
        a = _dot(xn_scr[...], w_ref[:, lo:hi]) + b_ref[:, lo:hi]
        g = _dot(xn_scr[...], w_ref[:, D + lo:D + hi]) + b_ref[:, D + lo:D + hi]
        u = a * jax.nn.sigmoid(g)
        for cg in range(lo // LANES, hi // LANES):
            o_ref[pl.ds(cg, TM, stride=CG), :] = u[:, cg * LANES - lo:(cg + 1) * LANES - lo]


def _pw1(h, mods, norm_g, w, b, layer, n_tok):
    return pl.pallas_call(
        _pw1_kernel,
        grid=(n_tok // TM,),
        in_specs=[
            pl.BlockSpec((TM, D), lambda i: (i, 0)),
            pl.BlockSpec((None, None, N_MOD, D), lambda i: (layer, _mod_row(TM)(i), 0, 0)),
            pl.BlockSpec((None, None, 1, D), lambda i: (layer, 0, 0, 0)),
            _resident((D, 2 * D)),
            _resident((1, 2 * D)),
        ],
        out_specs=pl.BlockSpec((TM * CG, LANES), lambda i: (i, 0)),
        out_shape=jax.ShapeDtypeStruct((n_tok * CG, LANES), f32),
        scratch_shapes=[pltpu.VMEM((TM, D), bf16)],
        compiler_params=_params("arbitrary"),
        name=f"conv_pw1_{layer}",
    )(h, mods, norm_g, w, b)


CONV_TB = 8


def _conv2_kernel(u_ref, up_ref, un_ref, h_ref, mod_ref, dww_ref, dwb_ref, lng_ref, lnb_ref,
                  w2_ref, b2_ref, o_ref, pad_scr, cv_scr):
    t = pl.program_id(0)
    tiles_per_seq = L // TL
    is_ctx = t >= T_LAT // TL
    first = jnp.logical_or(is_ctx, t % tiles_per_seq == 0)
    last = jnp.logical_or(is_ctx, t % tiles_per_seq == tiles_per_seq - 1)
    pad_scr[0:HALO] = jnp.where(first, 0.0, up_ref[...])
    pad_scr[HALO:HALO + TL] = u_ref[...]
    pad_scr[HALO + TL:HALO + TL + HALO] = jnp.where(last, 0.0, un_ref[...])

    base = HALO - CONV_HALF
    taps = [dww_ref[k] for k in range(CONV_W)]

    def conv_block(blk, carry):
        t0 = pl.multiple_of(blk * CONV_TB, CONV_TB)
        acc = [None] * CONV_TB
        for j in range(CONV_TB + CONV_W - 1):
            x = pad_scr[t0 + base + j]
            for i in range(CONV_TB):
                k = j - i
                if 0 <= k < CONV_W:
                    term = taps[k] * x
                    acc[i] = term if acc[i] is None else acc[i] + term
        for i in range(CONV_TB):
            cv_scr[pl.ds(pl.multiple_of((t0 + i) * CG, CG), CG), :] = acc[i]
        return carry

    lax.fori_loop(0, TL // CONV_TB, conv_block, 0)

    v = jnp.concatenate([cv_scr[pl.ds(g, TL, stride=CG), :] for g in range(CG)], axis=1)
    v = v + dwb_ref[...]
    mu = jnp.mean(v, axis=-1, keepdims=True)
    vc = v - mu
    var = jnp.mean(vc * vc, axis=-1, keepdims=True)
    y = vc * lax.rsqrt(var + EPS) * lng_ref[...] + lnb_ref[...]
    y = (y * jax.nn.sigmoid(y)).astype(bf16)
    z = _dot(y, w2_ref[...]) + b2_ref[...]
    o_ref[...] = h_ref[...] + mod_ref[2:3, :] * z


def _conv2(u, h, mods, dw_w, dw_b, ln_g, ln_b, w2, b2, layer, n_tok):
    nblk = n_tok // HALO
    per = TL // HALO
    u3 = u.reshape(n_tok, CG, LANES)
    return pl.pallas_call(
        _conv2_kernel,
        grid=(n_tok // TL,),
        in_specs=[
            pl.BlockSpec((TL, CG, LANES), lambda i: (i, 0, 0)),
            pl.BlockSpec((HALO, CG, LANES), lambda i: (jnp.maximum(i * per - 1, 0), 0, 0)),
            pl.BlockSpec((HALO, CG, LANES), lambda i: (jnp.minimum((i + 1) * per, nblk - 1), 0, 0)),
            pl.BlockSpec((TL, D), lambda i: (i, 0)),
            pl.BlockSpec((None, None, N_MOD, D), lambda i: (layer, _mod_row(TL)(i), 0, 0)),
            _resident((CONV_W, CG, LANES)),
            _resident((1, D)),
            _resident((1, D)),
            _resident((1, D)),
            _resident((D, D)),
            _resident((1, D)),
        ],
        out_specs=pl.BlockSpec((TL, D), lambda i: (i, 0)),
        out_shape=jax.ShapeDtypeStruct((n_tok, D), f32),
        scratch_shapes=[pltpu.VMEM((TL + 2 * HALO, CG, LANES), f32), pltpu.VMEM((TL * CG, LANES), f32)],
        compiler_params=_params("arbitrary"),
        name=f"conv_dw_pw2_{layer}",
    )(u3, u3, u3, h, mods, dw_w.reshape(CONV_W, CG, LANES), dw_b, ln_g, ln_b, w2, b2)


def _rope(x, cos, sin_signed, first_half):
    nxt = pltpu.roll(x, LANES - ROT // 2, axis=1)
    prv = pltpu.roll(x, ROT // 2, axis=1)
    return x * cos + jnp.where(first_half, nxt, prv) * sin_signed


def _qkv_kernel(h_ref, mod_ref, g_ref, w_ref, cos_ref, sin_ref, q_ref, k_ref, v_ref, xn_scr):
    xn_scr[...] = _norm_mod(h_ref[...], g_ref[...], mod_ref[0:1, :], mod_ref[1:2, :]).astype(bf16)
    cos = cos_ref[...]
    sin = sin_ref[...]
    lane = lax.broadcasted_iota(jnp.int32, (TM, LANES), 1)
    first_half = (lane % ROT) < (ROT // 2)
    q = _dot(xn_scr[...], w_ref[:, 0:D])
    for hd in range(H):
        lanes = slice(hd * DV, (hd + 1) * DV)
        q_ref[:, lanes] = (_rope(q[:, lanes], cos, sin, first_half) * Q_SCALE).astype(bf16)
    k = _dot(xn_scr[...], w_ref[:, D:2 * D])
    for hd in range(H):
        lanes = slice(hd * DV, (hd + 1) * DV)
        k_ref[:, lanes] = _rope(k[:, lanes], cos, sin, first_half).astype(bf16)
    v_ref[...] = _dot(xn_scr[...], w_ref[:, 2 * D:3 * D]).astype(bf16)


def _qkv(h, mods, norm_g, w, cos_tab, sin_tab, layer):
    lat_tiles = T_LAT // TM
    per_seq = L // TM
    tab_idx = lambda i: (jnp.where(i < lat_tiles, i % per_seq, per_seq), 0)
    out = jax.ShapeDtypeStruct((T_ALL, D), bf16)
    return pl.pallas_call(
        _qkv_kernel,
        grid=(T_ALL // TM,),
        in_specs=[
            pl.BlockSpec((TM, D), lambda i: (i, 0)),
            pl.BlockSpec((None, None, N_MOD, D), lambda i: (layer, _mod_row(TM)(i), 0, 0)),
            pl.BlockSpec((None, None, 1, D), lambda i: (layer, 0, 0, 0)),
            _resident((D, 3 * D)),
            pl.BlockSpec((TM, LANES), tab_idx),
            pl.BlockSpec((TM, LANES), tab_idx),
        ],
        out_specs=[pl.BlockSpec((TM, D), lambda i: (i, 0))] * 3,
        out_shape=[out, out, out],
        scratch_shapes=[pltpu.VMEM((TM, D), bf16)],
        compiler_params=_params("arbitrary"),
        name=f"attn_qkv_{layer}",
    )(h, mods, norm_g, w, cos_tab, sin_tab)


def _rope_tables():
    t = jnp.arange(L, dtype=f32)
    row = jnp.floor(t / GRID_W)
    col = t - row * GRID_W
    inv = ROPE_BASE ** (-2.0 * jnp.arange(ROT // 2, dtype=f32) / ROT)
    ang_r = row[:, None] * inv[None, :]
    ang_c = col[:, None] * inv[None, :]
    ang = jnp.concatenate([ang_r, ang_r, ang_c, ang_c], axis=-1)
    cos = jnp.cos(ang)
    sin = jnp.sin(ang)
    sign = jnp.where((jnp.arange(DH) % ROT) < ROT // 2, -1.0, 1.0).astype(f32)
    sin = sin * sign[None, :]
    cos = jnp.concatenate([jnp.tile(cos, (1, 2)), jnp.ones((TM, LANES), f32)], axis=0)
    sin = jnp.concatenate([jnp.tile(sin, (1, 2)), jnp.zeros((TM, LANES), f32)], axis=0)
    return cos, sin


def _attn_body(q_ref, k_refs, v_refs, lam_ref, g_ref, o_ref, lam_init, tq, tq_sub):
    lv = lam_ref[...]
    lam = (jnp.exp(jnp.sum(lv[0:1, :] * lv[1:2, :], axis=-1, keepdims=True))
           - jnp.exp(jnp.sum(lv[2:3, :] * lv[3:4, :], axis=-1, keepdims=True)) + lam_init)
    lane = lax.broadcasted_iota(jnp.int32, (tq_sub, LANES), 1)

    def scores(j):
        q = q_ref[j * tq_sub:(j + 1) * tq_sub, :]
        zero = jnp.zeros_like(q)
        qq = jnp.concatenate([jnp.where(lane < DH, q, zero), jnp.where(lane >= DH, q, zero)], axis=0)
        return [_dot_nt(qq, k_ref[...]) for k_ref in k_refs]

    vext = [jnp.concatenate([v_ref[...], jnp.ones(v_ref.shape, bf16)], axis=1) for v_ref in v_refs]

    def finish(s):
        m = functools.reduce(jnp.maximum, [jnp.max(x, axis=-1, keepdims=True) for x in s])
        r = None
        for x, ve in zip(s, vext):
            part = _dot(jnp.exp2(x - m).astype(bf16), ve)
            r = part if r is None else r + part
        on = r[:, :DV] / r[:, DV:]
        o = on[:tq_sub] - lam * on[tq_sub:]
        ms = jnp.mean(o * o, axis=-1, keepdims=True)
        o = o * lax.rsqrt(ms + EPS) * g_ref[...] * (1.0 - lam_init)
        return o.astype(bf16)

    n = tq // tq_sub
    s = scores(0)
    for j in range(n):
        s_next = scores(j + 1) if j + 1 < n else None
        o_ref[j * tq_sub:(j + 1) * tq_sub, :] = finish(s)
        s = s_next


def _attn_lat_kernel(q_ref, kc_ref, kl_ref, vc_ref, vl_ref, lam_ref, g_ref, o_ref, *, lam_init):
    _attn_body(q_ref, (kc_ref, kl_ref), (vc_ref, vl_ref), lam_ref, g_ref, o_ref, lam_init, TQ, TQ_SUB)


def _attn_ctx_kernel(o_in_ref, q_ref, kc_ref, vc_ref, lam_ref, g_ref, o_ref, *, lam_init):
    del o_in_ref
    _attn_body(q_ref, (kc_ref,), (vc_ref,), lam_ref, g_ref, o_ref, lam_init, LC, LC)


def _attention(q, k, v, lam_vec, subln_g, lam_init, with_ctx):
    nq = L // TQ
    ctx0 = T_LAT // LC
    q_spec = pl.BlockSpec((TQ, DV), lambda b, h, i: (b * nq + i, h))
    ctx_spec = pl.BlockSpec((LC, DV), lambda b, h, i: (ctx0 + b, h))
    lat_spec = pl.BlockSpec((L, DV), lambda b, h, i: (b, h))
    n_out = T_ALL if with_ctx else T_LAT
    o = pl.pallas_call(
        functools.partial(_attn_lat_kernel, lam_init=lam_init),
        grid=(B, H, nq),
        in_specs=[q_spec, ctx_spec, lat_spec, ctx_spec, lat_spec,
                  pl.BlockSpec((4, DH), lambda b, h, i: (0, 0)),
                  pl.BlockSpec((1, DV), lambda b, h, i: (0, 0))],
        out_specs=q_spec,
        out_shape=jax.ShapeDtypeStruct((n_out, D), bf16),
        compiler_params=_params("arbitrary", "arbitrary", "arbitrary"),
        name="diff_attn_latent",
    )(q, k, k, v, v, lam_vec, subln_g)
    if not with_ctx:
        return o
    cspec = pl.BlockSpec((LC, DV), lambda b, h: (ctx0 + b, h))
    return pl.pallas_call(
        functools.partial(_attn_ctx_kernel, lam_init=lam_init),
        grid=(B, H),
        in_specs=[pl.BlockSpec(memory_space=pl.ANY), cspec, cspec, cspec,
                  pl.BlockSpec((4, DH), lambda b, h: (0, 0)),
                  pl.BlockSpec((1, DV), lambda b, h: (0, 0))],
        out_specs=cspec,
        out_shape=jax.ShapeDtypeStruct((T_ALL, D), bf16),
        input_output_aliases={0: 0},
        compiler_params=_params("arbitrary", "arbitrary"),
        name="diff_attn_context",
    )(o, q, k, v, lam_vec, subln_g)


def kernel(x, c, ctx, c_ctx, mod_w, mod_b, norm_g, conv_pw1_w, conv_pw1_b, conv_dw_w, conv_dw_b, conv_ln_g, conv_ln_b, conv_pw2_w, conv_pw2_b, attn_w_qkv, attn_lambda, attn_subln_g, attn_w_o, ffn_w_in, ffn_w_out, final_g):
    assert x.shape == (B, L, D) and ctx.shape == (B, LC, D)
    h = jnp.concatenate([x.reshape(T_LAT, D), ctx.reshape(T_CTX, D)], axis=0)
    cvec = jnp.concatenate([c, c_ctx[None, :], jnp.zeros((MOD_ROWS - B - 1, D), f32)], axis=0)
    mods = _modulation(cvec, mod_w, mod_b).reshape(DEPTH, MOD_ROWS, N_MOD, D)
    norm_g4 = norm_g.reshape(DEPTH, 2, 1, D)
    cos_tab, sin_tab = _rope_tables()

    for i in range(DEPTH):
        with_ctx = i < DEPTH - 1
        n_tok = T_ALL if with_ctx else T_LAT
        j = i // N_MIXERS
        attn = None
        if i % N_MIXERS == 0:
            u = _pw1(h, mods, norm_g4, conv_pw1_w[j].astype(bf16), conv_pw1_b[j].reshape(1, 2 * D), i, n_tok)
            h = _conv2(u, h, mods, conv_dw_w[j], conv_dw_b[j].reshape(1, D), conv_ln_g[j].reshape(1, D),
                       conv_ln_b[j].reshape(1, D), conv_pw2_w[j].astype(bf16), conv_pw2_b[j].reshape(1, D),
                       i, n_tok)
        else:
            lam_init = 0.8 - 0.6 * math.exp(-0.3 * i)
            q, k, v = _qkv(h, mods, norm_g4, attn_w_qkv[j].astype(bf16), cos_tab, sin_tab, i)
            o = _attention(q, k, v, attn_lambda[j], attn_subln_g[j].reshape(1, DV), lam_init, with_ctx)
            attn = (o, attn_w_o[j].astype(bf16))
        h = _ffn(h, mods, norm_g4, ffn_w_in[i].astype(bf16), ffn_w_out[i].astype(bf16), i, n_tok, attn=attn,
                 final_g=final_g.reshape(1, D) if i == DEPTH - 1 else None)
    return h.reshape(B, L, D)
```

```python
import functools
import math

import jax
import jax.numpy as jnp
from jax import lax
from jax.experimental import pallas as pl
from jax.experimental.pallas import tpu as pltpu

D = 1024
B = 8
L = 2048
LC = 256
DEPTH = 4
GRID_W = 64
N_MIXERS = 2
CONV_W = 31
CONV_HALF = CONV_W // 2
DH = 64
DV = 2 * DH
H = D // DV
ROT = DH // 2
ROPE_BASE = 10000.0
FH = 2816
N_MOD = 6
EPS = 1e-6

LANES = 128
CG = D // LANES

T_LAT = B * L
T_CTX = B * LC
T_ALL = T_LAT + T_CTX
MOD_ROWS = 16
CTX_MOD_ROW = B

TM = 512
SEG = LC
HALO = 16
TQ_SUB = 256
Q_SCALE = math.log2(math.e) / math.sqrt(DH)
TF = 256
VMEM_LIMIT = 56 * 1024 * 1024

bf16 = jnp.bfloat16
f32 = jnp.float32


def _dot(a, b):
    return jnp.dot(a, b, preferred_element_type=f32)


def _dot_nt(a, b):
    return lax.dot_general(a, b, (((1,), (1,)), ((), ())), preferred_element_type=f32)


def _params(*sem):
    return pltpu.CompilerParams(dimension_semantics=sem, vmem_limit_bytes=VMEM_LIMIT)


def _resident(shape):
    nd = len(shape)
    return pl.BlockSpec(shape, lambda *_: (0,) * nd, pipeline_mode=pl.Buffered(1))


def _layer_resident(shape, layer):
    nd = len(shape)
    return pl.BlockSpec((None,) + tuple(shape), lambda *_: (layer,) + (0,) * nd, pipeline_mode=pl.Buffered(1))


def _mod_row(tile_rows):
    return lambda i: jnp.minimum((i * tile_rows) // L, CTX_MOD_ROW)


def _norm_mod(x, g, shift, scale):
    ms = jnp.mean(x * x, axis=-1, keepdims=True)
    y = x * lax.rsqrt(ms + EPS) * g
    return y * (1.0 + scale) + shift


MOD_TN = 1536


def _mod_kernel(c_ref, w_ref, b_ref, o_ref):
    c = c_ref[...]
    s = (c * jax.nn.sigmoid(c)).astype(bf16)
    o_ref[...] = _dot(s, w_ref[...].astype(bf16)) + b_ref[...]


def _modulation(cvec, mod_w, mod_b):
    n = N_MOD * D
    return pl.pallas_call(
        _mod_kernel,
        grid=(DEPTH, n // MOD_TN),
        in_specs=[
            pl.BlockSpec((MOD_ROWS, D), lambda i, j: (0, 0)),
            pl.BlockSpec((None, D, MOD_TN), lambda i, j: (i, 0, j)),
            pl.BlockSpec((None, 1, MOD_TN), lambda i, j: (i, 0, j)),
        ],
        out_specs=pl.BlockSpec((None, MOD_ROWS, MOD_TN), lambda i, j: (i, 0, j)),
        out_shape=jax.ShapeDtypeStruct((DEPTH, MOD_ROWS, n), f32),
        compiler_params=_params("arbitrary", "arbitrary"),
        name="modulation",
    )(cvec, mod_w, mod_b.reshape(DEPTH, 1, n))


def _swiglu_residual(x, mod_ref, g_ref, win_ref, wout_ref, xn_scr, a_scr, side_work=()):
    side_work = list(side_work)
    n_chunks = FH // TF
    per_chunk = -(-len(side_work) // n_chunks)
    xn_scr[...] = _norm_mod(x, g_ref[...], mod_ref[3:4, :], mod_ref[4:5, :]).astype(bf16)
    for c in range(n_chunks):
        gate = _dot(xn_scr[...], win_ref[:, c * TF:(c + 1) * TF])
        up = _dot(xn_scr[...], win_ref[:, FH + c * TF:FH + (c + 1) * TF])
        a_scr[:, c * TF:(c + 1) * TF] = (gate * jax.nn.sigmoid(gate) * up).astype(bf16)
        token = gate[0:CG, 0:LANES]
        for work in side_work[c * per_chunk:(c + 1) * per_chunk]:
            token = work(after=token)
    return x + mod_ref[5:6, :] * _dot(a_scr[...], wout_ref[...])


def _attn_ffn_kernel(*refs, with_ctx, final_out):
    h_ref, mod_ref, g_ref, win_ref, wout_ref, wo_ref, aol_ref = refs[:7]
    rest = list(refs[7:])
    aoc_ref = rest.pop(0) if with_ctx else None
    fg_ref = rest.pop(0) if final_out else None
    o_ref, xn_scr, a_scr = rest

    ao = aol_ref[...]
    if with_ctx:
        ao = jnp.where(pl.program_id(0) < T_LAT // TM, ao, aoc_ref[...])
    x = h_ref[...] + mod_ref[2:3, :] * _dot(ao, wo_ref[...])
    y = _swiglu_residual(x, mod_ref, g_ref, win_ref, wout_ref, xn_scr, a_scr)
    if final_out:
        ms = jnp.mean(y * y, axis=-1, keepdims=True)
        y = y * lax.rsqrt(ms + EPS) * fg_ref[...]
    o_ref[...] = y


def _attn_ffn(h, mods, norm_g, w_in, w_out, w_o, o_lat, o_ctx, layer, j, final_g=None):
    with_ctx = o_ctx is not None
    n_tok = T_ALL if with_ctx else T_LAT
    lat_tiles = T_LAT // TM
    row = lambda i: (i, 0)
    in_specs = [
        pl.BlockSpec((TM, D), row),
        pl.BlockSpec((None, None, N_MOD, D), lambda i: (layer, _mod_row(TM)(i), 0, 0)),
        pl.BlockSpec((None, None, 1, D), lambda i: (layer, 1, 0, 0)),
        _layer_resident((D, 2 * FH), layer),
        _layer_resident((FH, D), layer),
        _layer_resident((D, D), j),
        pl.BlockSpec((TM, D), lambda i: (jnp.minimum(i, lat_tiles - 1), 0)),
    ]
    args = [h, mods, norm_g, w_in, w_out, w_o, o_lat]
    if with_ctx:
        in_specs.append(pl.BlockSpec((TM, D), lambda i: (jnp.maximum(i - lat_tiles, 0), 0)))
        args.append(o_ctx)
    if final_g is not None:
        in_specs.append(_resident((1, D)))
        args.append(final_g)
    return pl.pallas_call(
        functools.partial(_attn_ffn_kernel, with_ctx=with_ctx, final_out=final_g is not None),
        grid=(n_tok // TM,),
        in_specs=in_specs,
        out_specs=pl.BlockSpec((TM, D), row),
        out_shape=jax.ShapeDtypeStruct((n_tok, D), f32),
        scratch_shapes=[pltpu.VMEM((TM, D), bf16), pltpu.VMEM((TM, FH), bf16)],
        compiler_params=_params("arbitrary"),
        name=f"attn_ffn{layer}",
    )(*args)


PW1_TN = 256


def _pw1_kernel(h_ref, mod_ref, g_ref, w_ref, b_ref, o_ref, xn_scr):
    xn_scr[...] = _norm_mod(h_ref[...], g_ref[...], mod_ref[0:1, :], mod_ref[1:2, :]).astype(bf16)
    for c in range(D // PW1_TN):
        lo, hi = c * PW1_TN, (c + 1) * PW1_TN
        a = _dot(xn_scr[...], w_ref[:, lo:hi]) + b_ref[:, lo:hi]
        g = _dot(xn_scr[...], w_ref[:, D + lo:D + hi]) + b_ref[:, D + lo:D + hi]
        u = a * jax.nn.sigmoid(g)
        for cg in range(lo // LANES, hi // LANES):
            o_ref[pl.ds(cg, TM, stride=CG), :] = u[:, cg * LANES - lo:(cg + 1) * LANES - lo]


def _pw1(h, mods, norm_g, w, b, layer, j, n_tok):
    return pl.pallas_call(
        _pw1_kernel,
        grid=(n_tok // TM,),
        in_specs=[
            pl.BlockSpec((TM, D), lambda i: (i, 0)),
            pl.BlockSpec((None, None, N_MOD, D), lambda i: (layer, _mod_row(TM)(i), 0, 0)),
            pl.BlockSpec((None, None, 1, D), lambda i: (layer, 0, 0, 0)),
            _layer_resident((D, 2 * D), j),
            _resident((1, 2 * D)),
        ],
        out_specs=pl.BlockSpec((TM * CG, LANES), lambda i: (i, 0)),
        out_shape=jax.ShapeDtypeStruct((n_tok * CG, LANES), f32),
        scratch_shapes=[pltpu.VMEM((TM, D), bf16)],
        compiler_params=_params("arbitrary"),
        name=f"conv_pw1_{layer}",
    )(h, mods, norm_g, w, b)


CONV_TB = 8


NSEG = TM // SEG
PAD_STEPS = SEG + 2 * HALO


def _fill_conv_pads(pad_scr, u_ref, left, right, is_ctx):
    for sg in range(NSEG):
        lo = sg * SEG
        before = left if sg == 0 else jnp.where(is_ctx, 0.0, u_ref[lo - HALO:lo])
        after = right if sg == NSEG - 1 else jnp.where(is_ctx, 0.0, u_ref[lo + SEG:lo + SEG + HALO])
        pad_scr[sg, 0:HALO] = before
        pad_scr[sg, HALO:HALO + SEG] = u_ref[lo:lo + SEG]
        pad_scr[sg, HALO + SEG:PAD_STEPS] = after


def _dwconv_block(pad_scr, sg, t0, dww_ref, cv_scr, aligned, after=None):
    taps = {}

    def tap(k):
        if k not in taps:
            taps[k] = dww_ref[k]
            if k == 0 and after is not None:
                always = jnp.logical_or(after == after, after != after)
                taps[k] = jnp.where(always, taps[k], 0.0)
        return taps[k]

    base = HALO - CONV_HALF
    acc = [None] * CONV_TB
    for j in range(CONV_TB + CONV_W - 1):
        x = pad_scr[sg, t0 + base + j]
        for i in range(CONV_TB):
            k = j - i
            if 0 <= k < CONV_W:
                term = tap(k) * x
                acc[i] = term if acc[i] is None else acc[i] + term
    for i in range(CONV_TB):
        cv_scr[pl.ds(aligned((sg * SEG + t0 + i) * CG), CG), :] = acc[i]
    return acc[CONV_TB - 1]


def _conv_ffn_kernel(u0_ref, u0n_ref, u_ref, up_ref, un_ref, h_ref, mod_ref, g_ref, dww_ref, dwb_ref,
                     lng_ref, lnb_ref, w2_ref, b2_ref, win_ref, wout_ref, o_ref,
                     pad_scr, cv_scr, xn_scr, a_scr, *, n_tiles):
    s = pl.program_id(0)
    tiles_per_seq = L // TM

    @pl.when(s == 0)
    def _():
        _fill_conv_pads(pad_scr, u0_ref, jnp.zeros((HALO, CG, LANES), f32), u0n_ref[...], False)
        for sg in range(NSEG):
            def body(blk, carry, sg=sg):
                t0 = pl.multiple_of(blk * CONV_TB, CONV_TB)
                _dwconv_block(pad_scr, sg, t0, dww_ref, cv_scr, lambda r: pl.multiple_of(r, CG))
                return carry
            lax.fori_loop(0, SEG // CONV_TB, body, 0)

    v = jnp.concatenate([cv_scr[pl.ds(g, TM, stride=CG), :] for g in range(CG)], axis=1)
    v = v + dwb_ref[...]
    mu = jnp.mean(v, axis=-1, keepdims=True)
    vc = v - mu
    var = jnp.mean(vc * vc, axis=-1, keepdims=True)
    y = vc * lax.rsqrt(var + EPS) * lng_ref[...] + lnb_ref[...]
    y = (y * jax.nn.sigmoid(y)).astype(bf16)
    x = h_ref[...] + mod_ref[2:3, :] * (_dot(y, w2_ref[...]) + b2_ref[...])

    t = jnp.minimum(s + 1, n_tiles - 1)
    is_ctx = t >= T_LAT // TM
    first = jnp.logical_or(is_ctx, t % tiles_per_seq == 0)
    last = jnp.logical_or(is_ctx, t % tiles_per_seq == tiles_per_seq - 1)
    _fill_conv_pads(pad_scr, u_ref, jnp.where(first, 0.0, up_ref[...]), jnp.where(last, 0.0, un_ref[...]), is_ctx)
    next_conv = [functools.partial(_dwconv_block, pad_scr, sg, blk * CONV_TB, dww_ref, cv_scr, lambda r: r)
                 for sg in range(NSEG) for blk in range(SEG // CONV_TB)]
    o_ref[...] = _swiglu_residual(x, mod_ref, g_ref, win_ref, wout_ref, xn_scr, a_scr, side_work=next_conv)


def _conv_ffn(u, h, mods, norm_g, dw_w, dw_b, ln_g, ln_b, w2, b2, w_in, w_out, layer, j):
    n_tiles = T_ALL // TM
    nblk = T_ALL // HALO
    per = TM // HALO
    nxt = lambda s: jnp.minimum(s + 1, n_tiles - 1)
    u3 = u.reshape(T_ALL, CG, LANES)
    tile = (TM, CG, LANES)
    halo = (HALO, CG, LANES)
    return pl.pallas_call(
        functools.partial(_conv_ffn_kernel, n_tiles=n_tiles),
        grid=(n_tiles,),
        in_specs=[
            pl.BlockSpec(tile, lambda s: (0, 0, 0), pipeline_mode=pl.Buffered(1)),
            pl.BlockSpec(halo, lambda s: (per, 0, 0), pipeline_mode=pl.Buffered(1)),
            pl.BlockSpec(tile, lambda s: (nxt(s), 0, 0)),
            pl.BlockSpec(halo, lambda s: (jnp.maximum(nxt(s) * per - 1, 0), 0, 0)),
            pl.BlockSpec(halo, lambda s: (jnp.minimum((nxt(s) + 1) * per, nblk - 1), 0, 0)),
            pl.BlockSpec((TM, D), lambda s: (s, 0)),
            pl.BlockSpec((None, None, N_MOD, D), lambda s: (layer, _mod_row(TM)(s), 0, 0)),
            pl.BlockSpec((None, None, 1, D), lambda s: (layer, 1, 0, 0)),
            _resident((CONV_W, CG, LANES)),
            _resident((1, D)),
            _resident((1, D)),
            _resident((1, D)),
            _layer_resident((D, D), j),
            _resident((1, D)),
            _layer_resident((D, 2 * FH), layer),
            _layer_resident((FH, D), layer),
        ],
        out_specs=pl.BlockSpec((TM, D), lambda s: (s, 0)),
        out_shape=jax.ShapeDtypeStruct((T_ALL, D), f32),
        scratch_shapes=[pltpu.VMEM((NSEG, PAD_STEPS, CG, LANES), f32), pltpu.VMEM((TM * CG, LANES), f32),
                        pltpu.VMEM((TM, D), bf16), pltpu.VMEM((TM, FH), bf16)],
        compiler_params=_params("arbitrary"),
        name=f"conv_ffn{layer}",
    )(u3, u3, u3, u3, u3, h, mods, norm_g, dw_w.reshape(CONV_W, CG, LANES), dw_b, ln_g, ln_b, w2, b2,
      w_in, w_out)


def _rope(x, cos, sin_signed, first_half):
    nxt = pltpu.roll(x, LANES - ROT // 2, axis=1)
    prv = pltpu.roll(x, ROT // 2, axis=1)
    return x * cos + jnp.where(first_half, nxt, prv) * sin_signed


def _qkv_kernel(h_ref, mod_ref, g_ref, w_ref, cos_ref, sin_ref, q_ref, k_ref, v_ref, xn_scr):
    xn_scr[...] = _norm_mod(h_ref[...], g_ref[...], mod_ref[0:1, :], mod_ref[1:2, :]).astype(bf16)
    cos = cos_ref[...]
    sin = sin_ref[...]
    lane = lax.broadcasted_iota(jnp.int32, (TM, LANES), 1)
    first_half = (lane % ROT) < (ROT // 2)
    q = _dot(xn_scr[...], w_ref[:, 0:D])
    for hd in range(H):
        lanes = slice(hd * DV, (hd + 1) * DV)
        q_ref[:, lanes] = (_rope(q[:, lanes], cos, sin, first_half) * Q_SCALE).astype(bf16)
    k = _dot(xn_scr[...], w_ref[:, D:2 * D])
    for hd in range(H):
        lanes = slice(hd * DV, (hd + 1) * DV)
        k_ref[:, lanes] = _rope(k[:, lanes], cos, sin, first_half).astype(bf16)
    v_ref[...] = _dot(xn_scr[...], w_ref[:, 2 * D:3 * D]).astype(bf16)


def _qkv(h, mods, norm_g, w, cos_tab, sin_tab, layer, j):
    lat_tiles = T_LAT // TM
    per_seq = L // TM
    tab_idx = lambda i: (jnp.where(i < lat_tiles, i % per_seq, per_seq), 0)
    out = jax.ShapeDtypeStruct((T_ALL, D), bf16)
    return pl.pallas_call(
        _qkv_kernel,
        grid=(T_ALL // TM,),
        in_specs=[
            pl.BlockSpec((TM, D), lambda i: (i, 0)),
            pl.BlockSpec((None, None, N_MOD, D), lambda i: (layer, _mod_row(TM)(i), 0, 0)),
            pl.BlockSpec((None, None, 1, D), lambda i: (layer, 0, 0, 0)),
            _layer_resident((D, 3 * D), j),
            pl.BlockSpec((TM, LANES), tab_idx),
            pl.BlockSpec((TM, LANES), tab_idx),
        ],
        out_specs=[pl.BlockSpec((TM, D), lambda i: (i, 0))] * 3,
        out_shape=[out, out, out],
        scratch_shapes=[pltpu.VMEM((TM, D), bf16)],
        compiler_params=_params("arbitrary"),
        name=f"attn_qkv_{layer}",
    )(h, mods, norm_g, w, cos_tab, sin_tab)


def _rope_tables():
    t = jnp.arange(L, dtype=f32)
    row = jnp.floor(t / GRID_W)
    col = t - row * GRID_W
    inv = ROPE_BASE ** (-2.0 * jnp.arange(ROT // 2, dtype=f32) / ROT)
    ang_r = row[:, None] * inv[None, :]
    ang_c = col[:, None] * inv[None, :]
    ang = jnp.concatenate([ang_r, ang_r, ang_c, ang_c], axis=-1)
    cos = jnp.cos(ang)
    sin = jnp.sin(ang)
    sign = jnp.where((jnp.arange(DH) % ROT) < ROT // 2, -1.0, 1.0).astype(f32)
    sin = sin * sign[None, :]
    cos = jnp.concatenate([jnp.tile(cos, (1, 2)), jnp.ones((TM, LANES), f32)], axis=0)
    sin = jnp.concatenate([jnp.tile(sin, (1, 2)), jnp.zeros((TM, LANES), f32)], axis=0)
    return cos, sin


def _attn_body(q_ref, k_refs, v_refs, lam_ref, g_ref, o_ref, lam_init, tq, tq_sub):
    lv = lam_ref[...]
    lam = (jnp.exp(jnp.sum(lv[0:1, :] * lv[1:2, :], axis=-1, keepdims=True))
           - jnp.exp(jnp.sum(lv[2:3, :] * lv[3:4, :], axis=-1, keepdims=True)) + lam_init)
    lane = lax.broadcasted_iota(jnp.int32, (tq_sub, LANES), 1)

    def scores(j):
        q = q_ref[j * tq_sub:(j + 1) * tq_sub, :]
        zero = jnp.zeros_like(q)
        qq = jnp.concatenate([jnp.where(lane < DH, q, zero), jnp.where(lane >= DH, q, zero)], axis=0)
        return [_dot_nt(qq, k_ref[...]) for k_ref in k_refs]

    vext = [jnp.concatenate([v_ref[...], jnp.ones(v_ref.shape, bf16)], axis=1) for v_ref in v_refs]

    def finish(s):
        m = functools.reduce(jnp.maximum, [jnp.max(x, axis=-1, keepdims=True) for x in s])
        r = None
        for x, ve in zip(s, vext):
            part = _dot(jnp.exp2(x - m).astype(bf16), ve)
            r = part if r is None else r + part
        on = r[:, :DV] / r[:, DV:]
        o = on[:tq_sub] - lam * on[tq_sub:]
        ms = jnp.mean(o * o, axis=-1, keepdims=True)
        o = o * lax.rsqrt(ms + EPS) * g_ref[...] * (1.0 - lam_init)
        return o.astype(bf16)

    n = tq // tq_sub
    s = scores(0)
    for j in range(n):
        s_next = scores(j + 1) if j + 1 < n else None
        o_ref[j * tq_sub:(j + 1) * tq_sub, :] = finish(s)
        s = s_next


def _attn_kernel(*refs, lam_init, with_ctx):
    if with_ctx:
        ql_ref, qc_ref, kc_ref, kl_ref, vc_ref, vl_ref, lam_ref, g_ref, ol_ref, oc_ref = refs
    else:
        ql_ref, kc_ref, kl_ref, vc_ref, vl_ref, lam_ref, g_ref, ol_ref = refs
    _attn_body(ql_ref, (kc_ref, kl_ref), (vc_ref, vl_ref), lam_ref, g_ref, ol_ref, lam_init, L, TQ_SUB)
    if with_ctx:
        _attn_body(qc_ref, (kc_ref,), (vc_ref,), lam_ref, g_ref, oc_ref, lam_init, LC, LC)


def _attention(q, k, v, lam_vec, subln_g, lam_init, with_ctx):
    ctx0 = T_LAT // LC
    lat_spec = pl.BlockSpec((L, DV), lambda b, h: (b, h))
    ctx_spec = pl.BlockSpec((LC, DV), lambda b, h: (ctx0 + b, h))
    small = [pl.BlockSpec((4, DH), lambda b, h: (0, 0)), pl.BlockSpec((1, DV), lambda b, h: (0, 0))]
    q_specs, q_args = ([lat_spec, ctx_spec], [q, q]) if with_ctx else ([lat_spec], [q])
    out_specs = [lat_spec]
    out_shape = [jax.ShapeDtypeStruct((T_LAT, D), bf16)]
    if with_ctx:
        out_specs.append(pl.BlockSpec((LC, DV), lambda b, h: (b, h)))
        out_shape.append(jax.ShapeDtypeStruct((T_CTX, D), bf16))
    return pl.pallas_call(
        functools.partial(_attn_kernel, lam_init=lam_init, with_ctx=with_ctx),
        grid=(B, H),
        in_specs=q_specs + [ctx_spec, lat_spec, ctx_spec, lat_spec] + small,
        out_specs=out_specs,
        out_shape=out_shape,
        compiler_params=_params("arbitrary", "arbitrary"),
        name="diff_attn",
    )(*q_args, k, k, v, v, lam_vec, subln_g)


def kernel(x, c, ctx, c_ctx, mod_w, mod_b, norm_g, conv_pw1_w, conv_pw1_b, conv_dw_w, conv_dw_b, conv_ln_g, conv_ln_b, conv_pw2_w, conv_pw2_b, attn_w_qkv, attn_lambda, attn_subln_g, attn_w_o, ffn_w_in, ffn_w_out, final_g):
    assert x.shape == (B, L, D) and ctx.shape == (B, LC, D)
    h = jnp.concatenate([x.reshape(T_LAT, D), ctx.reshape(T_CTX, D)], axis=0)
    cvec = jnp.concatenate([c, c_ctx[None, :], jnp.zeros((MOD_ROWS - B - 1, D), f32)], axis=0)
    mods = _modulation(cvec, mod_w, mod_b).reshape(DEPTH, MOD_ROWS, N_MOD, D)
    norm_g4 = norm_g.reshape(DEPTH, 2, 1, D)
    cos_tab, sin_tab = _rope_tables()
    w_in, w_out = ffn_w_in.astype(bf16), ffn_w_out.astype(bf16)
    w_pw1, w_pw2 = conv_pw1_w.astype(bf16), conv_pw2_w.astype(bf16)
    w_qkv, w_o = attn_w_qkv.astype(bf16), attn_w_o.astype(bf16)

    for i in range(DEPTH):
        with_ctx = i < DEPTH - 1
        j = i // N_MIXERS
        if i % N_MIXERS == 0:
            assert with_ctx
            u = _pw1(h, mods, norm_g4, w_pw1, conv_pw1_b[j].reshape(1, 2 * D), i, j, T_ALL)
            h = _conv_ffn(u, h, mods, norm_g4, conv_dw_w[j], conv_dw_b[j].reshape(1, D),
                          conv_ln_g[j].reshape(1, D), conv_ln_b[j].reshape(1, D), w_pw2,
                          conv_pw2_b[j].reshape(1, D), w_in, w_out, i, j)
        else:
            lam_init = 0.8 - 0.6 * math.exp(-0.3 * i)
            q, k, v = _qkv(h, mods, norm_g4, w_qkv, cos_tab, sin_tab, i, j)
            o = _attention(q, k, v, attn_lambda[j], attn_subln_g[j].reshape(1, DV), lam_init, with_ctx)
            o_lat, o_ctx = o if with_ctx else (o[0], None)
            h = _attn_ffn(h, mods, norm_g4, w_in, w_out, w_o, o_lat, o_ctx, i, j,
                          final_g=final_g.reshape(1, D) if i == DEPTH - 1 else None)
    return h.reshape(B, L, D)
```

```python
import functools
import math

import jax
import jax.numpy as jnp
import numpy as np
from jax import lax
from jax.experimental import pallas as pl
from jax.experimental.pallas import tpu as pltpu

D = 1024
B = 8
L = 2048
LC = 256
DEPTH = 4
GRID_W = 64
N_MIXERS = 2
CONV_W = 31
CONV_HALF = CONV_W // 2
DH = 64
DV = 2 * DH
H = D // DV
ROT = DH // 2
ROPE_BASE = 10000.0
FH = 2816
N_MOD = 6
EPS = 1e-6

LANES = 128
CG = D // LANES

T_LAT = B * L
T_CTX = B * LC
T_ALL = T_LAT + T_CTX
MOD_ROWS = 16
CTX_MOD_ROW = B

TM = 512
SEG = LC
HALO = 16
TQ_SUB = 256
Q_SCALE = math.log2(math.e) / math.sqrt(DH)
TF = 256
VMEM_LIMIT = 56 * 1024 * 1024

bf16 = jnp.bfloat16
f32 = jnp.float32


def _dot(a, b):
    return jnp.dot(a, b, preferred_element_type=f32)


def _dot_nt(a, b):
    return lax.dot_general(a, b, (((1,), (1,)), ((), ())), preferred_element_type=f32)


def _params(*sem):
    return pltpu.CompilerParams(dimension_semantics=sem, vmem_limit_bytes=VMEM_LIMIT)


def _resident(shape):
    nd = len(shape)
    return pl.BlockSpec(shape, lambda *_: (0,) * nd, pipeline_mode=pl.Buffered(1))


def _layer_resident(shape, layer):
    nd = len(shape)
    return pl.BlockSpec((None,) + tuple(shape), lambda *_: (layer,) + (0,) * nd, pipeline_mode=pl.Buffered(1))


def _token_specs(tokens):
    if not isinstance(tokens, tuple):
        return [pl.BlockSpec((TM, D), lambda i: (i, 0))], [tokens]
    lat_tiles = T_LAT // TM
    return ([pl.BlockSpec((TM, D), lambda i: (jnp.minimum(i, lat_tiles - 1), 0)),
             pl.BlockSpec((TM, D), lambda i: (jnp.maximum(i - lat_tiles, 0), 0))], list(tokens))


def _load_tokens(refs):
    if len(refs) == 1:
        return refs[0][...]
    lat_ref, ctx_ref = refs
    return jnp.where(pl.program_id(0) < T_LAT // TM, lat_ref[...], ctx_ref[...])


def _mod_row(tile_rows):
    return lambda i: jnp.minimum((i * tile_rows) // L, CTX_MOD_ROW)


def _norm_mod(x, g, shift, scale):
    ms = jnp.mean(x * x, axis=-1, keepdims=True)
    y = x * lax.rsqrt(ms + EPS) * g
    return y * (1.0 + scale) + shift


MOD_TN = 1536


def _mod_kernel(c_ref, w_ref, b_ref, o_ref):
    c = c_ref[...]
    s = (c * jax.nn.sigmoid(c)).astype(bf16)
    o_ref[...] = _dot(s, w_ref[...].astype(bf16)) + b_ref[...]


def _modulation(cvec, mod_w, mod_b):
    n = N_MOD * D
    return pl.pallas_call(
        _mod_kernel,
        grid=(DEPTH, n // MOD_TN),
        in_specs=[
            pl.BlockSpec((MOD_ROWS, D), lambda i, j: (0, 0)),
            pl.BlockSpec((None, D, MOD_TN), lambda i, j: (i, 0, j)),
            pl.BlockSpec((None, 1, MOD_TN), lambda i, j: (i, 0, j)),
        ],
        out_specs=pl.BlockSpec((None, MOD_ROWS, MOD_TN), lambda i, j: (i, 0, j)),
        out_shape=jax.ShapeDtypeStruct((DEPTH, MOD_ROWS, n), f32),
        compiler_params=_params("arbitrary", "arbitrary"),
        name="modulation",
    )(cvec, mod_w, mod_b.reshape(DEPTH, 1, n))


def _swiglu_residual(x, mod_ref, g_ref, win_ref, wout_ref, xn_scr, a_scr, side_work=()):
    side_work = list(side_work)
    n_chunks = FH // TF
    per_chunk = -(-len(side_work) // n_chunks)
    xn_scr[...] = _norm_mod(x, g_ref[...], mod_ref[3:4, :], mod_ref[4:5, :]).astype(bf16)
    for c in range(n_chunks):
        gate = _dot(xn_scr[...], win_ref[:, c * TF:(c + 1) * TF])
        up = _dot(xn_scr[...], win_ref[:, FH + c * TF:FH + (c + 1) * TF])
        a_scr[:, c * TF:(c + 1) * TF] = (gate * jax.nn.sigmoid(gate) * up).astype(bf16)
        token = gate[0:CG, 0:LANES]
        for work in side_work[c * per_chunk:(c + 1) * per_chunk]:
            token = work(after=token)
    return x + mod_ref[5:6, :] * _dot(a_scr[...], wout_ref[...])


def _attn_ffn_kernel(*refs, with_ctx, final_out):
    h_ref, mod_ref, g_ref, win_ref, wout_ref, wo_ref, aol_ref = refs[:7]
    rest = list(refs[7:])
    aoc_ref = rest.pop(0) if with_ctx else None
    fg_ref = rest.pop(0) if final_out else None
    o_ref, xn_scr, a_scr = rest

    ao = aol_ref[...]
    if with_ctx:
        ao = jnp.where(pl.program_id(0) < T_LAT // TM, ao, aoc_ref[...])
    x = h_ref[...] + mod_ref[2:3, :] * _dot(ao, wo_ref[...])
    y = _swiglu_residual(x, mod_ref, g_ref, win_ref, wout_ref, xn_scr, a_scr)
    if final_out:
        ms = jnp.mean(y * y, axis=-1, keepdims=True)
        y = y * lax.rsqrt(ms + EPS) * fg_ref[...]
    o_ref[...] = y


def _attn_ffn(h, mods, norm_g, w_in, w_out, w_o, o_lat, o_ctx, layer, j, final_g=None):
    with_ctx = o_ctx is not None
    n_tok = T_ALL if with_ctx else T_LAT
    lat_tiles = T_LAT // TM
    row = lambda i: (i, 0)
    in_specs = [
        pl.BlockSpec((TM, D), row),
        pl.BlockSpec((None, None, N_MOD, D), lambda i: (layer, _mod_row(TM)(i), 0, 0)),
        pl.BlockSpec((None, None, 1, D), lambda i: (layer, 1, 0, 0)),
        _layer_resident((D, 2 * FH), layer),
        _layer_resident((FH, D), layer),
        _layer_resident((D, D), j),
        pl.BlockSpec((TM, D), lambda i: (jnp.minimum(i, lat_tiles - 1), 0)),
    ]
    args = [h, mods, norm_g, w_in, w_out, w_o, o_lat]
    if with_ctx:
        in_specs.append(pl.BlockSpec((TM, D), lambda i: (jnp.maximum(i - lat_tiles, 0), 0)))
        args.append(o_ctx)
    if final_g is not None:
        in_specs.append(_resident((1, D)))
        args.append(final_g)
    return pl.pallas_call(
        functools.partial(_attn_ffn_kernel, with_ctx=with_ctx, final_out=final_g is not None),
        grid=(n_tok // TM,),
        in_specs=in_specs,
        out_specs=pl.BlockSpec((TM, D), row),
        out_shape=jax.ShapeDtypeStruct((n_tok, D), f32),
        scratch_shapes=[pltpu.VMEM((TM, D), bf16), pltpu.VMEM((TM, FH), bf16)],
        compiler_params=_params("arbitrary"),
        name=f"attn_ffn{layer}",
    )(*args)


PW1_TN = 256


def _pw1_kernel(mod_ref, g_ref, w_ref, b_ref, *rest, n_h):
    o_ref, xn_scr = rest[n_h:]
    xn_scr[...] = _norm_mod(_load_tokens(rest[:n_h]), g_ref[...], mod_ref[0:1, :], mod_ref[1:2, :]).astype(bf16)
    for c in range(D // PW1_TN):
        lo, hi = c * PW1_TN, (c + 1) * PW1_TN
        a = _dot(xn_scr[...], w_ref[:, lo:hi]) + b_ref[:, lo:hi]
        g = _dot(xn_scr[...], w_ref[:, D + lo:D + hi]) + b_ref[:, D + lo:D + hi]
        u = a * jax.nn.sigmoid(g)
        for cg in range(lo // LANES, hi // LANES):
            o_ref[pl.ds(cg, TM, stride=CG), :] = u[:, cg * LANES - lo:(cg + 1) * LANES - lo]


def _pw1(h, mods, norm_g, w, b, layer, j, n_tok):
    h_specs, h_args = _token_specs(h)
    return pl.pallas_call(
        functools.partial(_pw1_kernel, n_h=len(h_args)),
        grid=(n_tok // TM,),
        in_specs=[
            pl.BlockSpec((None, None, N_MOD, D), lambda i: (layer, _mod_row(TM)(i), 0, 0)),
            pl.BlockSpec((None, None, 1, D), lambda i: (layer, 0, 0, 0)),
            _layer_resident((D, 2 * D), j),
            _resident((1, 2 * D)),
        ] + h_specs,
        out_specs=pl.BlockSpec((TM * CG, LANES), lambda i: (i, 0)),
        out_shape=jax.ShapeDtypeStruct((n_tok * CG, LANES), f32),
        scratch_shapes=[pltpu.VMEM((TM, D), bf16)],
        compiler_params=_params("arbitrary"),
        name=f"conv_pw1_{layer}",
    )(mods, norm_g, w, b, *h_args)


CONV_TB = 8


NSEG = TM // SEG
PAD_STEPS = SEG + 2 * HALO


def _fill_conv_pads(pad_scr, u_ref, left, right, is_ctx):
    for sg in range(NSEG):
        lo = sg * SEG
        before = left if sg == 0 else jnp.where(is_ctx, 0.0, u_ref[lo - HALO:lo])
        after = right if sg == NSEG - 1 else jnp.where(is_ctx, 0.0, u_ref[lo + SEG:lo + SEG + HALO])
        pad_scr[sg, 0:HALO] = before
        pad_scr[sg, HALO:HALO + SEG] = u_ref[lo:lo + SEG]
        pad_scr[sg, HALO + SEG:PAD_STEPS] = after


def _dwconv_block(pad_scr, sg, t0, dww_ref, cv_scr, aligned, after=None):
    taps = {}

    def tap(k):
        if k not in taps:
            taps[k] = dww_ref[k]
            if k == 0 and after is not None:
                always = jnp.logical_or(after == after, after != after)
                taps[k] = jnp.where(always, taps[k], 0.0)
        return taps[k]

    base = HALO - CONV_HALF
    acc = [None] * CONV_TB
    for j in range(CONV_TB + CONV_W - 1):
        x = pad_scr[sg, t0 + base + j]
        for i in range(CONV_TB):
            k = j - i
            if 0 <= k < CONV_W:
                term = tap(k) * x
                acc[i] = term if acc[i] is None else acc[i] + term
    for i in range(CONV_TB):
        cv_scr[pl.ds(aligned((sg * SEG + t0 + i) * CG), CG), :] = acc[i]
    return acc[CONV_TB - 1]


def _conv_ffn_kernel(u0_ref, u0n_ref, u_ref, up_ref, un_ref, mod_ref, g_ref, dww_ref, dwb_ref,
                     lng_ref, lnb_ref, w2_ref, b2_ref, win_ref, wout_ref, *rest, n_tiles, n_h):
    h_refs = rest[:n_h]
    o_ref, pad_scr, cv_scr, xn_scr, a_scr = rest[n_h:]
    s = pl.program_id(0)
    tiles_per_seq = L // TM

    @pl.when(s == 0)
    def _():
        _fill_conv_pads(pad_scr, u0_ref, jnp.zeros((HALO, CG, LANES), f32), u0n_ref[...], False)
        for sg in range(NSEG):
            def body(blk, carry, sg=sg):
                t0 = pl.multiple_of(blk * CONV_TB, CONV_TB)
                _dwconv_block(pad_scr, sg, t0, dww_ref, cv_scr, lambda r: pl.multiple_of(r, CG))
                return carry
            lax.fori_loop(0, SEG // CONV_TB, body, 0)

    v = jnp.concatenate([cv_scr[pl.ds(g, TM, stride=CG), :] for g in range(CG)], axis=1)
    v = v + dwb_ref[...]
    mu = jnp.mean(v, axis=-1, keepdims=True)
    vc = v - mu
    var = jnp.mean(vc * vc, axis=-1, keepdims=True)
    y = vc * lax.rsqrt(var + EPS) * lng_ref[...] + lnb_ref[...]
    y = (y * jax.nn.sigmoid(y)).astype(bf16)
    x = _load_tokens(h_refs) + mod_ref[2:3, :] * (_dot(y, w2_ref[...]) + b2_ref[...])

    t = jnp.minimum(s + 1, n_tiles - 1)
    is_ctx = t >= T_LAT // TM
    first = jnp.logical_or(is_ctx, t % tiles_per_seq == 0)
    last = jnp.logical_or(is_ctx, t % tiles_per_seq == tiles_per_seq - 1)
    _fill_conv_pads(pad_scr, u_ref, jnp.where(first, 0.0, up_ref[...]), jnp.where(last, 0.0, un_ref[...]), is_ctx)
    next_conv = [functools.partial(_dwconv_block, pad_scr, sg, blk * CONV_TB, dww_ref, cv_scr, lambda r: r)
                 for sg in range(NSEG) for blk in range(SEG // CONV_TB)]
    o_ref[...] = _swiglu_residual(x, mod_ref, g_ref, win_ref, wout_ref, xn_scr, a_scr, side_work=next_conv)


def _conv_ffn(u, h, mods, norm_g, dw_w, dw_b, ln_g, ln_b, w2, b2, w_in, w_out, layer, j):
    n_tiles = T_ALL // TM
    nblk = T_ALL // HALO
    per = TM // HALO
    nxt = lambda s: jnp.minimum(s + 1, n_tiles - 1)
    u3 = u.reshape(T_ALL, CG, LANES)
    tile = (TM, CG, LANES)
    halo = (HALO, CG, LANES)
    h_specs, h_args = _token_specs(h)
    return pl.pallas_call(
        functools.partial(_conv_ffn_kernel, n_tiles=n_tiles, n_h=len(h_args)),
        grid=(n_tiles,),
        in_specs=[
            pl.BlockSpec(tile, lambda s: (0, 0, 0), pipeline_mode=pl.Buffered(1)),
            pl.BlockSpec(halo, lambda s: (per, 0, 0), pipeline_mode=pl.Buffered(1)),
            pl.BlockSpec(tile, lambda s: (nxt(s), 0, 0)),
            pl.BlockSpec(halo, lambda s: (jnp.maximum(nxt(s) * per - 1, 0), 0, 0)),
            pl.BlockSpec(halo, lambda s: (jnp.minimum((nxt(s) + 1) * per, nblk - 1), 0, 0)),
            pl.BlockSpec((None, None, N_MOD, D), lambda s: (layer, _mod_row(TM)(s), 0, 0)),
            pl.BlockSpec((None, None, 1, D), lambda s: (layer, 1, 0, 0)),
            _resident((CONV_W, CG, LANES)),
            _resident((1, D)),
            _resident((1, D)),
            _resident((1, D)),
            _layer_resident((D, D), j),
            _resident((1, D)),
            _layer_resident((D, 2 * FH), layer),
            _layer_resident((FH, D), layer),
        ] + h_specs,
        out_specs=pl.BlockSpec((TM, D), lambda s: (s, 0)),
        out_shape=jax.ShapeDtypeStruct((T_ALL, D), f32),
        scratch_shapes=[pltpu.VMEM((NSEG, PAD_STEPS, CG, LANES), f32), pltpu.VMEM((TM * CG, LANES), f32),
                        pltpu.VMEM((TM, D), bf16), pltpu.VMEM((TM, FH), bf16)],
        compiler_params=_params("arbitrary"),
        name=f"conv_ffn{layer}",
    )(u3, u3, u3, u3, u3, mods, norm_g, dw_w.reshape(CONV_W, CG, LANES), dw_b, ln_g, ln_b, w2, b2,
      w_in, w_out, *h_args)


def _rope(x, cos, sin_signed, first_half):
    nxt = pltpu.roll(x, LANES - ROT // 2, axis=1)
    prv = pltpu.roll(x, ROT // 2, axis=1)
    return x * cos + jnp.where(first_half, nxt, prv) * sin_signed


def _qkv_kernel(h_ref, mod_ref, g_ref, w_ref, cos_ref, sin_ref, q_ref, k_ref, v_ref, xn_scr):
    xn_scr[...] = _norm_mod(h_ref[...], g_ref[...], mod_ref[0:1, :], mod_ref[1:2, :]).astype(bf16)
    cos = cos_ref[...]
    sin = sin_ref[...]
    lane = lax.broadcasted_iota(jnp.int32, (TM, LANES), 1)
    first_half = (lane % ROT) < (ROT // 2)
    q = _dot(xn_scr[...], w_ref[:, 0:D])
    for hd in range(H):
        lanes = slice(hd * DV, (hd + 1) * DV)
        q_ref[:, lanes] = (_rope(q[:, lanes], cos, sin, first_half) * Q_SCALE).astype(bf16)
    k = _dot(xn_scr[...], w_ref[:, D:2 * D])
    for hd in range(H):
        lanes = slice(hd * DV, (hd + 1) * DV)
        k_ref[:, lanes] = _rope(k[:, lanes], cos, sin, first_half).astype(bf16)
    v_ref[...] = _dot(xn_scr[...], w_ref[:, 2 * D:3 * D]).astype(bf16)


def _qkv(h, mods, norm_g, w, cos_tab, sin_tab, layer, j):
    lat_tiles = T_LAT // TM
    per_seq = L // TM
    tab_idx = lambda i: (jnp.where(i < lat_tiles, i % per_seq, per_seq), 0)
    out = jax.ShapeDtypeStruct((T_ALL, D), bf16)
    return pl.pallas_call(
        _qkv_kernel,
        grid=(T_ALL // TM,),
        in_specs=[
            pl.BlockSpec((TM, D), lambda i: (i, 0)),
            pl.BlockSpec((None, None, N_MOD, D), lambda i: (layer, _mod_row(TM)(i), 0, 0)),
            pl.BlockSpec((None, None, 1, D), lambda i: (layer, 0, 0, 0)),
            _layer_resident((D, 3 * D), j),
            pl.BlockSpec((TM, LANES), tab_idx),
            pl.BlockSpec((TM, LANES), tab_idx),
        ],
        out_specs=[pl.BlockSpec((TM, D), lambda i: (i, 0))] * 3,
        out_shape=[out, out, out],
        scratch_shapes=[pltpu.VMEM((TM, D), bf16)],
        compiler_params=_params("arbitrary"),
        name=f"attn_qkv_{layer}",
    )(h, mods, norm_g, w, cos_tab, sin_tab)


def _rope_tables():
    t = np.arange(L, dtype=np.float32)
    row = np.floor(t / GRID_W)
    col = t - row * GRID_W
    inv = (ROPE_BASE ** (-2.0 * np.arange(ROT // 2, dtype=np.float32) / ROT)).astype(np.float32)
    ang_r = row[:, None] * inv[None, :]
    ang_c = col[:, None] * inv[None, :]
    ang = np.concatenate([ang_r, ang_r, ang_c, ang_c], axis=-1)
    sign = np.where((np.arange(DH) % ROT) < ROT // 2, -1.0, 1.0).astype(np.float32)
    cos = np.cos(ang).astype(np.float32)
    sin = np.sin(ang).astype(np.float32) * sign[None, :]
    cos = np.concatenate([np.tile(cos, (1, 2)), np.ones((TM, LANES), np.float32)], axis=0)
    sin = np.concatenate([np.tile(sin, (1, 2)), np.zeros((TM, LANES), np.float32)], axis=0)
    return jnp.asarray(cos), jnp.asarray(sin)


def _attn_body(q_ref, k_refs, v_refs, lam_ref, g_ref, o_ref, lam_init, tq, tq_sub):
    lv = lam_ref[...]
    lam = (jnp.exp(jnp.sum(lv[0:1, :] * lv[1:2, :], axis=-1, keepdims=True))
           - jnp.exp(jnp.sum(lv[2:3, :] * lv[3:4, :], axis=-1, keepdims=True)) + lam_init)
    lane = lax.broadcasted_iota(jnp.int32, (tq_sub, LANES), 1)

    def scores(j):
        q = q_ref[j * tq_sub:(j + 1) * tq_sub, :]
        zero = jnp.zeros_like(q)
        qq = jnp.concatenate([jnp.where(lane < DH, q, zero), jnp.where(lane >= DH, q, zero)], axis=0)
        return [_dot_nt(qq, k_ref[...]) for k_ref in k_refs]

    vext = [jnp.concatenate([v_ref[...], jnp.ones(v_ref.shape, bf16)], axis=1) for v_ref in v_refs]

    def finish(s):
        m = functools.reduce(jnp.maximum, [jnp.max(x, axis=-1, keepdims=True) for x in s])
        r = None
        for x, ve in zip(s, vext):
            part = _dot(jnp.exp2(x - m).astype(bf16), ve)
            r = part if r is None else r + part
        on = r[:, :DV] / r[:, DV:]
        o = on[:tq_sub] - lam * on[tq_sub:]
        ms = jnp.mean(o * o, axis=-1, keepdims=True)
        o = o * lax.rsqrt(ms + EPS) * g_ref[...] * (1.0 - lam_init)
        return o.astype(bf16)

    n = tq // tq_sub
    s = scores(0)
    for j in range(n):
        s_next = scores(j + 1) if j + 1 < n else None
        o_ref[j * tq_sub:(j + 1) * tq_sub, :] = finish(s)
        s = s_next


def _attn_kernel(*refs, lam_init, with_ctx):
    if with_ctx:
        ql_ref, qc_ref, kc_ref, kl_ref, vc_ref, vl_ref, lam_ref, g_ref, ol_ref, oc_ref = refs
    else:
        ql_ref, kc_ref, kl_ref, vc_ref, vl_ref, lam_ref, g_ref, ol_ref = refs
    _attn_body(ql_ref, (kc_ref, kl_ref), (vc_ref, vl_ref), lam_ref, g_ref, ol_ref, lam_init, L, TQ_SUB)
    if with_ctx:
        _attn_body(qc_ref, (kc_ref,), (vc_ref,), lam_ref, g_ref, oc_ref, lam_init, LC, LC)


def _attention(q, k, v, lam_vec, subln_g, lam_init, with_ctx):
    ctx0 = T_LAT // LC
    lat_spec = pl.BlockSpec((L, DV), lambda b, h: (b, h))
    ctx_spec = pl.BlockSpec((LC, DV), lambda b, h: (ctx0 + b, h))
    small = [pl.BlockSpec((4, DH), lambda b, h: (0, 0)), pl.BlockSpec((1, DV), lambda b, h: (0, 0))]
    q_specs, q_args = ([lat_spec, ctx_spec], [q, q]) if with_ctx else ([lat_spec], [q])
    out_specs = [lat_spec]
    out_shape = [jax.ShapeDtypeStruct((T_LAT, D), bf16)]
    if with_ctx:
        out_specs.append(pl.BlockSpec((LC, DV), lambda b, h: (b, h)))
        out_shape.append(jax.ShapeDtypeStruct((T_CTX, D), bf16))
    return pl.pallas_call(
        functools.partial(_attn_kernel, lam_init=lam_init, with_ctx=with_ctx),
        grid=(B, H),
        in_specs=q_specs + [ctx_spec, lat_spec, ctx_spec, lat_spec] + small,
        out_specs=out_specs,
        out_shape=out_shape,
        compiler_params=_params("arbitrary", "arbitrary"),
        name="diff_attn",
    )(*q_args, k, k, v, v, lam_vec, subln_g)


def kernel(x, c, ctx, c_ctx, mod_w, mod_b, norm_g, conv_pw1_w, conv_pw1_b, conv_dw_w, conv_dw_b, conv_ln_g, conv_ln_b, conv_pw2_w, conv_pw2_b, attn_w_qkv, attn_lambda, attn_subln_g, attn_w_o, ffn_w_in, ffn_w_out, final_g):
    assert x.shape == (B, L, D) and ctx.shape == (B, LC, D)
    h = (x.reshape(T_LAT, D), ctx.reshape(T_CTX, D))
    cvec = jnp.concatenate([c, c_ctx[None, :], jnp.zeros((MOD_ROWS - B - 1, D), f32)], axis=0)
    mods = _modulation(cvec, mod_w, mod_b).reshape(DEPTH, MOD_ROWS, N_MOD, D)
    norm_g4 = norm_g.reshape(DEPTH, 2, 1, D)
    cos_tab, sin_tab = _rope_tables()
    w_in, w_out = ffn_w_in.astype(bf16), ffn_w_out.astype(bf16)
    w_pw1, w_pw2 = conv_pw1_w.astype(bf16), conv_pw2_w.astype(bf16)
    w_qkv, w_o = attn_w_qkv.astype(bf16), attn_w_o.astype(bf16)

    for i in range(DEPTH):
        with_ctx = i < DEPTH - 1
        j = i // N_MIXERS
        if i % N_MIXERS == 0:
            assert with_ctx
            u = _pw1(h, mods, norm_g4, w_pw1, conv_pw1_b[j].reshape(1, 2 * D), i, j, T_ALL)
            h = _conv_ffn(u, h, mods, norm_g4, conv_dw_w[j], conv_dw_b[j].reshape(1, D),
                          conv_ln_g[j].reshape(1, D), conv_ln_b[j].reshape(1, D), w_pw2,
                          conv_pw2_b[j].reshape(1, D), w_in, w_out, i, j)
        else:
            lam_init = 0.8 - 0.6 * math.exp(-0.3 * i)
            q, k, v = _qkv(h, mods, norm_g4, w_qkv, cos_tab, sin_tab, i, j)
            o = _attention(q, k, v, attn_lambda[j], attn_subln_g[j].reshape(1, DV), lam_init, with_ctx)
            o_lat, o_ctx = o if with_ctx else (o[0], None)
            h = _attn_ffn(h, mods, norm_g4, w_in, w_out, w_o, o_lat, o_ctx, i, j,
                          final_g=final_g.reshape(1, D) if i == DEPTH - 1 else None)
    return h.reshape(B, L, D)
```

```python
import functools
import math

import jax
import jax.numpy as jnp
import numpy as np
from jax import lax
from jax.experimental import pallas as pl
from jax.experimental.pallas import tpu as pltpu

D = 1024
B = 8
L = 2048
LC = 256
DEPTH = 4
GRID_W = 64
N_MIXERS = 2
CONV_W = 31
CONV_HALF = CONV_W // 2
DH = 64
DV = 2 * DH
H = D // DV
ROT = DH // 2
ROPE_BASE = 10000.0
FH = 2816
N_MOD = 6
EPS = 1e-6

LANES = 128
CG = D // LANES

T_LAT = B * L
T_CTX = B * LC
T_ALL = T_LAT + T_CTX
MOD_ROWS = 16
CTX_MOD_ROW = B

TM = 512
TMP = 1024
SEG = LC
HALO = 16
TQ_SUB = 256
HP = 2
Q_SCALE = math.log2(math.e) / math.sqrt(DH)
TF = 256
VMEM_LIMIT = 56 * 1024 * 1024

bf16 = jnp.bfloat16
f32 = jnp.float32


def _dot(a, b):
    return jnp.dot(a, b, preferred_element_type=f32)


def _dot_nt(a, b):
    return lax.dot_general(a, b, (((1,), (1,)), ((), ())), preferred_element_type=f32)


def _params(*sem):
    return pltpu.CompilerParams(dimension_semantics=sem, vmem_limit_bytes=VMEM_LIMIT)


def _resident(shape):
    nd = len(shape)
    return pl.BlockSpec(shape, lambda *_: (0,) * nd, pipeline_mode=pl.Buffered(1))


def _layer_resident(shape, layer):
    nd = len(shape)
    return pl.BlockSpec((None,) + tuple(shape), lambda *_: (layer,) + (0,) * nd, pipeline_mode=pl.Buffered(1))


def _token_specs(tokens, tm):
    if not isinstance(tokens, tuple):
        return [pl.BlockSpec((tm, D), lambda i: (i, 0))], [tokens]
    lat_tiles = T_LAT // tm
    return ([pl.BlockSpec((tm, D), lambda i: (jnp.minimum(i, lat_tiles - 1), 0)),
             pl.BlockSpec((tm, D), lambda i: (jnp.maximum(i - lat_tiles, 0), 0))], list(tokens))


def _load_tokens(refs, tm):
    if len(refs) == 1:
        return refs[0][...]
    lat_ref, ctx_ref = refs
    return jnp.where(pl.program_id(0) < T_LAT // tm, lat_ref[...], ctx_ref[...])


def _mod_row(tile_rows):
    return lambda i: jnp.minimum((i * tile_rows) // L, CTX_MOD_ROW)


def _norm_mod(x, g, shift, scale):
    ms = jnp.mean(x * x, axis=-1, keepdims=True)
    y = x * lax.rsqrt(ms + EPS) * g
    return y * (1.0 + scale) + shift


MOD_TN = 1536


def _mod_kernel(c_ref, w_ref, b_ref, o_ref):
    c = c_ref[...]
    s = (c * jax.nn.sigmoid(c)).astype(bf16)
    o_ref[...] = _dot(s, w_ref[...].astype(bf16)) + b_ref[...]


def _modulation(cvec, mod_w, mod_b):
    n = N_MOD * D
    return pl.pallas_call(
        _mod_kernel,
        grid=(DEPTH, n // MOD_TN),
        in_specs=[
            pl.BlockSpec((MOD_ROWS, D), lambda i, j: (0, 0)),
            pl.BlockSpec((None, D, MOD_TN), lambda i, j: (i, 0, j)),
            pl.BlockSpec((None, 1, MOD_TN), lambda i, j: (i, 0, j)),
        ],
        out_specs=pl.BlockSpec((None, MOD_ROWS, MOD_TN), lambda i, j: (i, 0, j)),
        out_shape=jax.ShapeDtypeStruct((DEPTH, MOD_ROWS, n), f32),
        compiler_params=_params("arbitrary", "arbitrary"),
        name="modulation",
    )(cvec, mod_w, mod_b.reshape(DEPTH, 1, n))


def _swiglu_residual(x, mod_ref, g_ref, win_ref, wout_ref, xn_scr, a_scr, side_work=()):
    side_work = list(side_work)
    n_chunks = FH // TF
    per_chunk = -(-len(side_work) // n_chunks)
    xn_scr[...] = _norm_mod(x, g_ref[...], mod_ref[3:4, :], mod_ref[4:5, :]).astype(bf16)
    for c in range(n_chunks):
        gate = _dot(xn_scr[...], win_ref[:, c * TF:(c + 1) * TF])
        up = _dot(xn_scr[...], win_ref[:, FH + c * TF:FH + (c + 1) * TF])
        a_scr[:, c * TF:(c + 1) * TF] = (gate * jax.nn.sigmoid(gate) * up).astype(bf16)
        token = gate[0:CG, 0:LANES]
        for work in side_work[c * per_chunk:(c + 1) * per_chunk]:
            token = work(after=token)
    return x + mod_ref[5:6, :] * _dot(a_scr[...], wout_ref[...])


def _attn_ffn_kernel(*refs, with_ctx, final_out):
    h_ref, mod_ref, g_ref, win_ref, wout_ref, wo_ref, aol_ref = refs[:7]
    rest = list(refs[7:])
    aoc_ref = rest.pop(0) if with_ctx else None
    fg_ref = rest.pop(0) if final_out else None
    o_ref, xn_scr, a_scr = rest

    ao = aol_ref[...]
    if with_ctx:
        ao = jnp.where(pl.program_id(0) < T_LAT // TM, ao, aoc_ref[...])
    x = h_ref[...] + mod_ref[2:3, :] * _dot(ao, wo_ref[...])
    y = _swiglu_residual(x, mod_ref, g_ref, win_ref, wout_ref, xn_scr, a_scr)
    if final_out:
        ms = jnp.mean(y * y, axis=-1, keepdims=True)
        y = y * lax.rsqrt(ms + EPS) * fg_ref[...]
    o_ref[...] = y


def _attn_ffn(h, mods, norm_g, w_in, w_out, w_o, o_lat, o_ctx, layer, j, final_g=None):
    with_ctx = o_ctx is not None
    n_tok = T_ALL if with_ctx else T_LAT
    lat_tiles = T_LAT // TM
    row = lambda i: (i, 0)
    in_specs = [
        pl.BlockSpec((TM, D), row),
        pl.BlockSpec((None, None, N_MOD, D), lambda i: (layer, _mod_row(TM)(i), 0, 0)),
        pl.BlockSpec((None, None, 1, D), lambda i: (layer, 1, 0, 0)),
        _layer_resident((D, 2 * FH), layer),
        _layer_resident((FH, D), layer),
        _layer_resident((D, D), j),
        pl.BlockSpec((TM, D), lambda i: (jnp.minimum(i, lat_tiles - 1), 0)),
    ]
    args = [h, mods, norm_g, w_in, w_out, w_o, o_lat]
    if with_ctx:
        in_specs.append(pl.BlockSpec((TM, D), lambda i: (jnp.maximum(i - lat_tiles, 0), 0)))
        args.append(o_ctx)
    if final_g is not None:
        in_specs.append(_resident((1, D)))
        args.append(final_g)
    return pl.pallas_call(
        functools.partial(_attn_ffn_kernel, with_ctx=with_ctx, final_out=final_g is not None),
        grid=(n_tok // TM,),
        in_specs=in_specs,
        out_specs=pl.BlockSpec((TM, D), row),
        out_shape=jax.ShapeDtypeStruct((n_tok, D), f32),
        scratch_shapes=[pltpu.VMEM((TM, D), bf16), pltpu.VMEM((TM, FH), bf16)],
        compiler_params=_params("arbitrary"),
        name=f"attn_ffn{layer}",
    )(*args)


PW1_TN = 256


def _pw1_kernel(mod_ref, g_ref, w_ref, b_ref, *rest, n_h):
    o_ref, xn_scr = rest[n_h:]
    h = _load_tokens(rest[:n_h], TMP)
    xn_scr[...] = _norm_mod(h, g_ref[...], mod_ref[0:1, :], mod_ref[1:2, :]).astype(bf16)
    for c in range(D // PW1_TN):
        lo, hi = c * PW1_TN, (c + 1) * PW1_TN
        a = _dot(xn_scr[...], w_ref[:, lo:hi]) + b_ref[:, lo:hi]
        g = _dot(xn_scr[...], w_ref[:, D + lo:D + hi]) + b_ref[:, D + lo:D + hi]
        u = a * jax.nn.sigmoid(g)
        for cg in range(lo // LANES, hi // LANES):
            o_ref[pl.ds(cg, TMP, stride=CG), :] = u[:, cg * LANES - lo:(cg + 1) * LANES - lo]


def _pw1(h, mods, norm_g, w, b, layer, j, n_tok):
    h_specs, h_args = _token_specs(h, TMP)
    return pl.pallas_call(
        functools.partial(_pw1_kernel, n_h=len(h_args)),
        grid=(n_tok // TMP,),
        in_specs=[
            pl.BlockSpec((None, None, N_MOD, D), lambda i: (layer, _mod_row(TMP)(i), 0, 0)),
            pl.BlockSpec((None, None, 1, D), lambda i: (layer, 0, 0, 0)),
            _layer_resident((D, 2 * D), j),
            _resident((1, 2 * D)),
        ] + h_specs,
        out_specs=pl.BlockSpec((TMP * CG, LANES), lambda i: (i, 0)),
        out_shape=jax.ShapeDtypeStruct((n_tok * CG, LANES), f32),
        scratch_shapes=[pltpu.VMEM((TMP, D), bf16)],
        compiler_params=_params("arbitrary"),
        name=f"conv_pw1_{layer}",
    )(mods, norm_g, w, b, *h_args)


CONV_TB = 8


NSEG = TM // SEG
PAD_STEPS = SEG + 2 * HALO


def _fill_conv_pads(pad_scr, u_ref, left, right, is_ctx):
    for sg in range(NSEG):
        lo = sg * SEG
        before = left if sg == 0 else jnp.where(is_ctx, 0.0, u_ref[lo - HALO:lo])
        after = right if sg == NSEG - 1 else jnp.where(is_ctx, 0.0, u_ref[lo + SEG:lo + SEG + HALO])
        pad_scr[sg, 0:HALO] = before
        pad_scr[sg, HALO:HALO + SEG] = u_ref[lo:lo + SEG]
        pad_scr[sg, HALO + SEG:PAD_STEPS] = after


def _dwconv_block(pad_scr, sg, t0, dww_ref, cv_scr, aligned, after=None):
    taps = {}

    def tap(k):
        if k not in taps:
            taps[k] = dww_ref[k]
            if k == 0 and after is not None:
                always = jnp.logical_or(after == after, after != after)
                taps[k] = jnp.where(always, taps[k], 0.0)
        return taps[k]

    base = HALO - CONV_HALF
    acc = [None] * CONV_TB
    for j in range(CONV_TB + CONV_W - 1):
        x = pad_scr[sg, t0 + base + j]
        for i in range(CONV_TB):
            k = j - i
            if 0 <= k < CONV_W:
                term = tap(k) * x
                acc[i] = term if acc[i] is None else acc[i] + term
    for i in range(CONV_TB):
        cv_scr[pl.ds(aligned((sg * SEG + t0 + i) * CG), CG), :] = acc[i]
    return acc[CONV_TB - 1]


def _conv_ffn_kernel(u0_ref, u0n_ref, u_ref, up_ref, un_ref, mod_ref, g_ref, dww_ref, dwb_ref,
                     lng_ref, lnb_ref, w2_ref, b2_ref, win_ref, wout_ref, *rest, n_tiles, n_h):
    h_refs = rest[:n_h]
    o_ref, pad_scr, cv_scr, xn_scr, a_scr = rest[n_h:]
    s = pl.program_id(0)
    tiles_per_seq = L // TM

    @pl.when(s == 0)
    def _():
        _fill_conv_pads(pad_scr, u0_ref, jnp.zeros((HALO, CG, LANES), f32), u0n_ref[...], False)
        for sg in range(NSEG):
            def body(blk, carry, sg=sg):
                t0 = pl.multiple_of(blk * CONV_TB, CONV_TB)
                _dwconv_block(pad_scr, sg, t0, dww_ref, cv_scr, lambda r: pl.multiple_of(r, CG))
                return carry
            lax.fori_loop(0, SEG // CONV_TB, body, 0)

    v = jnp.concatenate([cv_scr[pl.ds(g, TM, stride=CG), :] for g in range(CG)], axis=1)
    v = v + dwb_ref[...]
    mu = jnp.mean(v, axis=-1, keepdims=True)
    vc = v - mu
    var = jnp.mean(vc * vc, axis=-1, keepdims=True)
    y = vc * lax.rsqrt(var + EPS) * lng_ref[...] + lnb_ref[...]
    y = (y * jax.nn.sigmoid(y)).astype(bf16)
    x = _load_tokens(h_refs, TM) + mod_ref[2:3, :] * (_dot(y, w2_ref[...]) + b2_ref[...])

    t = jnp.minimum(s + 1, n_tiles - 1)
    is_ctx = t >= T_LAT // TM
    first = jnp.logical_or(is_ctx, t % tiles_per_seq == 0)
    last = jnp.logical_or(is_ctx, t % tiles_per_seq == tiles_per_seq - 1)
    _fill_conv_pads(pad_scr, u_ref, jnp.where(first, 0.0, up_ref[...]), jnp.where(last, 0.0, un_ref[...]), is_ctx)
    next_conv = [functools.partial(_dwconv_block, pad_scr, sg, blk * CONV_TB, dww_ref, cv_scr, lambda r: r)
                 for sg in range(NSEG) for blk in range(SEG // CONV_TB)]
    o_ref[...] = _swiglu_residual(x, mod_ref, g_ref, win_ref, wout_ref, xn_scr, a_scr, side_work=next_conv)


def _conv_ffn(u, h, mods, norm_g, dw_w, dw_b, ln_g, ln_b, w2, b2, w_in, w_out, layer, j):
    n_tiles = T_ALL // TM
    nblk = T_ALL // HALO
    per = TM // HALO
    nxt = lambda s: jnp.minimum(s + 1, n_tiles - 1)
    u3 = u.reshape(T_ALL, CG, LANES)
    tile = (TM, CG, LANES)
    halo = (HALO, CG, LANES)
    h_specs, h_args = _token_specs(h, TM)
    return pl.pallas_call(
        functools.partial(_conv_ffn_kernel, n_tiles=n_tiles, n_h=len(h_args)),
        grid=(n_tiles,),
        in_specs=[
            pl.BlockSpec(tile, lambda s: (0, 0, 0), pipeline_mode=pl.Buffered(1)),
            pl.BlockSpec(halo, lambda s: (per, 0, 0), pipeline_mode=pl.Buffered(1)),
            pl.BlockSpec(tile, lambda s: (nxt(s), 0, 0)),
            pl.BlockSpec(halo, lambda s: (jnp.maximum(nxt(s) * per - 1, 0), 0, 0)),
            pl.BlockSpec(halo, lambda s: (jnp.minimum((nxt(s) + 1) * per, nblk - 1), 0, 0)),
            pl.BlockSpec((None, None, N_MOD, D), lambda s: (layer, _mod_row(TM)(s), 0, 0)),
            pl.BlockSpec((None, None, 1, D), lambda s: (layer, 1, 0, 0)),
            _resident((CONV_W, CG, LANES)),
            _resident((1, D)),
            _resident((1, D)),
            _resident((1, D)),
            _layer_resident((D, D), j),
            _resident((1, D)),
            _layer_resident((D, 2 * FH), layer),
            _layer_resident((FH, D), layer),
        ] + h_specs,
        out_specs=pl.BlockSpec((TM, D), lambda s: (s, 0)),
        out_shape=jax.ShapeDtypeStruct((T_ALL, D), f32),
        scratch_shapes=[pltpu.VMEM((NSEG, PAD_STEPS, CG, LANES), f32), pltpu.VMEM((TM * CG, LANES), f32),
                        pltpu.VMEM((TM, D), bf16), pltpu.VMEM((TM, FH), bf16)],
        compiler_params=_params("arbitrary"),
        name=f"conv_ffn{layer}",
    )(u3, u3, u3, u3, u3, mods, norm_g, dw_w.reshape(CONV_W, CG, LANES), dw_b, ln_g, ln_b, w2, b2,
      w_in, w_out, *h_args)


def _rope(x, cos, sin_signed, first_half):
    nxt = pltpu.roll(x, LANES - ROT // 2, axis=1)
    prv = pltpu.roll(x, ROT // 2, axis=1)
    return x * cos + jnp.where(first_half, nxt, prv) * sin_signed


def _qkv_kernel(h_ref, mod_ref, g_ref, w_ref, cos_ref, sin_ref, q_ref, k_ref, v_ref, xn_scr):
    xn_scr[...] = _norm_mod(h_ref[...], g_ref[...], mod_ref[0:1, :], mod_ref[1:2, :]).astype(bf16)
    cos = cos_ref[...]
    sin = sin_ref[...]
    lane = lax.broadcasted_iota(jnp.int32, (TMP, LANES), 1)
    first_half = (lane % ROT) < (ROT // 2)
    q = _dot(xn_scr[...], w_ref[:, 0:D])
    for hd in range(H):
        lanes = slice(hd * DV, (hd + 1) * DV)
        q_ref[:, lanes] = (_rope(q[:, lanes], cos, sin, first_half) * Q_SCALE).astype(bf16)
    k = _dot(xn_scr[...], w_ref[:, D:2 * D])
    for hd in range(H):
        lanes = slice(hd * DV, (hd + 1) * DV)
        k_ref[:, lanes] = _rope(k[:, lanes], cos, sin, first_half).astype(bf16)
    v_ref[...] = _dot(xn_scr[...], w_ref[:, 2 * D:3 * D]).astype(bf16)


def _qkv(h, mods, norm_g, w, cos_tab, sin_tab, layer, j):
    lat_tiles = T_LAT // TMP
    per_seq = L // TMP
    tab_idx = lambda i: (jnp.where(i < lat_tiles, i % per_seq, per_seq), 0)
    out = jax.ShapeDtypeStruct((T_ALL, D), bf16)
    return pl.pallas_call(
        _qkv_kernel,
        grid=(T_ALL // TMP,),
        in_specs=[
            pl.BlockSpec((TMP, D), lambda i: (i, 0)),
            pl.BlockSpec((None, None, N_MOD, D), lambda i: (layer, _mod_row(TMP)(i), 0, 0)),
            pl.BlockSpec((None, None, 1, D), lambda i: (layer, 0, 0, 0)),
            _layer_resident((D, 3 * D), j),
            pl.BlockSpec((TMP, LANES), tab_idx),
            pl.BlockSpec((TMP, LANES), tab_idx),
        ],
        out_specs=[pl.BlockSpec((TMP, D), lambda i: (i, 0))] * 3,
        out_shape=[out, out, out],
        scratch_shapes=[pltpu.VMEM((TMP, D), bf16)],
        compiler_params=_params("arbitrary"),
        name=f"attn_qkv_{layer}",
    )(h, mods, norm_g, w, cos_tab, sin_tab)


def _rope_tables():
    t = np.arange(L, dtype=np.float32)
    row = np.floor(t / GRID_W)
    col = t - row * GRID_W
    inv = (ROPE_BASE ** (-2.0 * np.arange(ROT // 2, dtype=np.float32) / ROT)).astype(np.float32)
    ang_r = row[:, None] * inv[None, :]
    ang_c = col[:, None] * inv[None, :]
    ang = np.concatenate([ang_r, ang_r, ang_c, ang_c], axis=-1)
    sign = np.where((np.arange(DH) % ROT) < ROT // 2, -1.0, 1.0).astype(np.float32)
    cos = np.cos(ang).astype(np.float32)
    sin = np.sin(ang).astype(np.float32) * sign[None, :]
    cos = np.concatenate([np.tile(cos, (1, 2)), np.ones((TMP, LANES), np.float32)], axis=0)
    sin = np.concatenate([np.tile(sin, (1, 2)), np.zeros((TMP, LANES), np.float32)], axis=0)
    return jnp.asarray(cos), jnp.asarray(sin)


def _attn_body(q_ref, k_refs, v_refs, lam_ref, g_ref, o_ref, lam_init, tq, tq_sub, lanes):
    lv = lam_ref[...]
    lam = (jnp.exp(jnp.sum(lv[0:1, :] * lv[1:2, :], axis=-1, keepdims=True))
           - jnp.exp(jnp.sum(lv[2:3, :] * lv[3:4, :], axis=-1, keepdims=True)) + lam_init)
    lane = lax.broadcasted_iota(jnp.int32, (tq_sub, LANES), 1)

    def scores(j):
        q = q_ref[j * tq_sub:(j + 1) * tq_sub, lanes]
        zero = jnp.zeros_like(q)
        qq = jnp.concatenate([jnp.where(lane < DH, q, zero), jnp.where(lane >= DH, q, zero)], axis=0)
        return [_dot_nt(qq, k_ref[:, lanes]) for k_ref in k_refs]

    vext = [jnp.concatenate([v_ref[:, lanes], jnp.ones((v_ref.shape[0], DV), bf16)], axis=1) for v_ref in v_refs]

    def finish(s):
        m = functools.reduce(jnp.maximum, [jnp.max(x, axis=-1, keepdims=True) for x in s])
        r = None
        for x, ve in zip(s, vext):
            part = _dot(jnp.exp2(x - m).astype(bf16), ve)
            r = part if r is None else r + part
        on = r[:, :DV] / r[:, DV:]
        o = on[:tq_sub] - lam * on[tq_sub:]
        ms = jnp.mean(o * o, axis=-1, keepdims=True)
        o = o * lax.rsqrt(ms + EPS) * g_ref[...] * (1.0 - lam_init)
        return o.astype(bf16)

    n = tq // tq_sub
    s = scores(0)
    for j in range(n):
        s_next = scores(j + 1) if j + 1 < n else None
        o_ref[j * tq_sub:(j + 1) * tq_sub, lanes] = finish(s)
        s = s_next


def _attn_kernel(*refs, lam_init, with_ctx):
    if with_ctx:
        ql_ref, qc_ref, kc_ref, kl_ref, vc_ref, vl_ref, lam_ref, g_ref, ol_ref, oc_ref = refs
    else:
        ql_ref, kc_ref, kl_ref, vc_ref, vl_ref, lam_ref, g_ref, ol_ref = refs
    for hh in range(HP):
        lanes = slice(hh * DV, (hh + 1) * DV)
        _attn_body(ql_ref, (kc_ref, kl_ref), (vc_ref, vl_ref), lam_ref, g_ref, ol_ref, lam_init, L, TQ_SUB, lanes)
        if with_ctx:
            _attn_body(qc_ref, (kc_ref,), (vc_ref,), lam_ref, g_ref, oc_ref, lam_init, LC, LC, lanes)


def _attention(q, k, v, lam_vec, subln_g, lam_init, with_ctx):
    ctx0 = T_LAT // LC
    lat_spec = pl.BlockSpec((L, HP * DV), lambda b, h: (b, h))
    ctx_spec = pl.BlockSpec((LC, HP * DV), lambda b, h: (ctx0 + b, h))
    small = [pl.BlockSpec((4, DH), lambda b, h: (0, 0)), pl.BlockSpec((1, DV), lambda b, h: (0, 0))]
    q_specs, q_args = ([lat_spec, ctx_spec], [q, q]) if with_ctx else ([lat_spec], [q])
    out_specs = [lat_spec]
    out_shape = [jax.ShapeDtypeStruct((T_LAT, D), bf16)]
    if with_ctx:
        out_specs.append(pl.BlockSpec((LC, HP * DV), lambda b, h: (b, h)))
        out_shape.append(jax.ShapeDtypeStruct((T_CTX, D), bf16))
    return pl.pallas_call(
        functools.partial(_attn_kernel, lam_init=lam_init, with_ctx=with_ctx),
        grid=(B, H // HP),
        in_specs=q_specs + [ctx_spec, lat_spec, ctx_spec, lat_spec] + small,
        out_specs=out_specs,
        out_shape=out_shape,
        compiler_params=_params("arbitrary", "arbitrary"),
        name="diff_attn",
    )(*q_args, k, k, v, v, lam_vec, subln_g)


def kernel(x, c, ctx, c_ctx, mod_w, mod_b, norm_g, conv_pw1_w, conv_pw1_b, conv_dw_w, conv_dw_b, conv_ln_g, conv_ln_b, conv_pw2_w, conv_pw2_b, attn_w_qkv, attn_lambda, attn_subln_g, attn_w_o, ffn_w_in, ffn_w_out, final_g):
    assert x.shape == (B, L, D) and ctx.shape == (B, LC, D)
    h = (x.reshape(T_LAT, D), ctx.reshape(T_CTX, D))
    cvec = jnp.concatenate([c, c_ctx[None, :], jnp.zeros((MOD_ROWS - B - 1, D), f32)], axis=0)
    mods = _modulation(cvec, mod_w, mod_b).reshape(DEPTH, MOD_ROWS, N_MOD, D)
    norm_g4 = norm_g.reshape(DEPTH, 2, 1, D)
    cos_tab, sin_tab = _rope_tables()
    w_in, w_out = ffn_w_in.astype(bf16), ffn_w_out.astype(bf16)
    w_pw1, w_pw2 = conv_pw1_w.astype(bf16), conv_pw2_w.astype(bf16)
    w_qkv, w_o = attn_w_qkv.astype(bf16), attn_w_o.astype(bf16)

    for i in range(DEPTH):
        with_ctx = i < DEPTH - 1
        j = i // N_MIXERS
        if i % N_MIXERS == 0:
            assert with_ctx
            u = _pw1(h, mods, norm_g4, w_pw1, conv_pw1_b[j].reshape(1, 2 * D), i, j, T_ALL)
            h = _conv_ffn(u, h, mods, norm_g4, conv_dw_w[j], conv_dw_b[j].reshape(1, D),
                          conv_ln_g[j].reshape(1, D), conv_ln_b[j].reshape(1, D), w_pw2,
                          conv_pw2_b[j].reshape(1, D), w_in, w_out, i, j)
        else:
            lam_init = 0.8 - 0.6 * math.exp(-0.3 * i)
            q, k, v = _qkv(h, mods, norm_g4, w_qkv, cos_tab, sin_tab, i, j)
            o = _attention(q, k, v, attn_lambda[j], attn_subln_g[j].reshape(1, DV), lam_init, with_ctx)
            o_lat, o_ctx = o if with_ctx else (o[0], None)
            h = _attn_ffn(h, mods, norm_g4, w_in, w_out, w_o, o_lat, o_ctx, i, j,
                          final_g=final_g.reshape(1, D) if i == DEPTH - 1 else None)
    return h.reshape(B, L, D)
```

```python
import functools
import math

import jax
import jax.numpy as jnp
import numpy as np
from jax import lax
from jax.experimental import pallas as pl
from jax.experimental.pallas import tpu as pltpu

D = 1024
B = 8
L = 2048
LC = 256
DEPTH = 4
GRID_W = 64
N_MIXERS = 2
CONV_W = 31
CONV_HALF = CONV_W // 2
DH = 64
DV = 2 * DH
H = D // DV
ROT = DH // 2
ROPE_BASE = 10000.0
FH = 2816
N_MOD = 6
EPS = 1e-6

LANES = 128
CG = D // LANES

T_LAT = B * L
T_CTX = B * LC
T_ALL = T_LAT + T_CTX
MOD_ROWS = 16
CTX_MOD_ROW = B

TM = 512
TMP = 1024
SEG = LC
HALO = 16
TQ_SUB = 256
HP = 2
Q_SCALE = math.log2(math.e) / math.sqrt(DH)
TF = 256
VMEM_LIMIT = 56 * 1024 * 1024

bf16 = jnp.bfloat16
f32 = jnp.float32


def _dot(a, b):
    return jnp.dot(a, b, preferred_element_type=f32)


def _dot_nt(a, b):
    return lax.dot_general(a, b, (((1,), (1,)), ((), ())), preferred_element_type=f32)


def _params(*sem):
    return pltpu.CompilerParams(dimension_semantics=sem, vmem_limit_bytes=VMEM_LIMIT)


def _resident(shape):
    nd = len(shape)
    return pl.BlockSpec(shape, lambda *_: (0,) * nd, pipeline_mode=pl.Buffered(1))


def _layer_resident(shape, layer):
    nd = len(shape)
    return pl.BlockSpec((None,) + tuple(shape), lambda *_: (layer,) + (0,) * nd, pipeline_mode=pl.Buffered(1))


def _token_specs(tokens, tm):
    if not isinstance(tokens, tuple):
        return [pl.BlockSpec((tm, D), lambda i: (i, 0))], [tokens]
    lat_tiles = T_LAT // tm
    return ([pl.BlockSpec((tm, D), lambda i: (jnp.minimum(i, lat_tiles - 1), 0)),
             pl.BlockSpec((tm, D), lambda i: (jnp.maximum(i - lat_tiles, 0), 0))], list(tokens))


def _load_tokens(refs, tm):
    if len(refs) == 1:
        return refs[0][...]
    lat_ref, ctx_ref = refs
    return jnp.where(pl.program_id(0) < T_LAT // tm, lat_ref[...], ctx_ref[...])


def _mod_row(tile_rows):
    return lambda i: jnp.minimum((i * tile_rows) // L, CTX_MOD_ROW)


def _norm_mod(x, g, shift, scale):
    ms = jnp.mean(x * x, axis=-1, keepdims=True)
    y = x * lax.rsqrt(ms + EPS) * g
    return y * (1.0 + scale) + shift


MOD_TN = 1536


def _mod_kernel(c_ref, w_ref, b_ref, o_ref):
    c = c_ref[...]
    s = (c * jax.nn.sigmoid(c)).astype(bf16)
    o_ref[...] = _dot(s, w_ref[...].astype(bf16)) + b_ref[...]


def _modulation(cvec, mod_w, mod_b):
    n = N_MOD * D
    return pl.pallas_call(
        _mod_kernel,
        grid=(DEPTH, n // MOD_TN),
        in_specs=[
            pl.BlockSpec((MOD_ROWS, D), lambda i, j: (0, 0)),
            pl.BlockSpec((None, D, MOD_TN), lambda i, j: (i, 0, j)),
            pl.BlockSpec((None, 1, MOD_TN), lambda i, j: (i, 0, j)),
        ],
        out_specs=pl.BlockSpec((None, MOD_ROWS, MOD_TN), lambda i, j: (i, 0, j)),
        out_shape=jax.ShapeDtypeStruct((DEPTH, MOD_ROWS, n), f32),
        compiler_params=_params("arbitrary", "arbitrary"),
        name="modulation",
    )(cvec, mod_w, mod_b.reshape(DEPTH, 1, n))


def _swiglu_residual(x, mod_ref, g_ref, win_ref, wout_ref, xn_scr, a_scr, side_work=()):
    side_work = list(side_work)
    n_chunks = FH // TF
    per_chunk = -(-len(side_work) // n_chunks)
    xn_scr[...] = _norm_mod(x, g_ref[...], mod_ref[3:4, :], mod_ref[4:5, :]).astype(bf16)
    for c in range(n_chunks):
        gate = _dot(xn_scr[...], win_ref[:, c * TF:(c + 1) * TF])
        up = _dot(xn_scr[...], win_ref[:, FH + c * TF:FH + (c + 1) * TF])
        act = gate * jax.nn.sigmoid(gate) * up
        a_scr[:, c * TF:(c + 1) * TF] = act.astype(bf16)
        token = act[x.shape[0] - CG:, TF - LANES:]
        for work in side_work[c * per_chunk:(c + 1) * per_chunk]:
            token = work(after=token)
    return x + mod_ref[5:6, :] * _dot(a_scr[...], wout_ref[...])


def _attn_ffn_kernel(*refs, with_ctx, final_out):
    h_ref, mod_ref, g_ref, win_ref, wout_ref, wo_ref, aol_ref = refs[:7]
    rest = list(refs[7:])
    aoc_ref = rest.pop(0) if with_ctx else None
    fg_ref = rest.pop(0) if final_out else None
    o_ref, xn_scr, a_scr = rest

    ao = aol_ref[...]
    if with_ctx:
        ao = jnp.where(pl.program_id(0) < T_LAT // TM, ao, aoc_ref[...])
    x = h_ref[...] + mod_ref[2:3, :] * _dot(ao, wo_ref[...])
    y = _swiglu_residual(x, mod_ref, g_ref, win_ref, wout_ref, xn_scr, a_scr)
    if final_out:
        ms = jnp.mean(y * y, axis=-1, keepdims=True)
        y = y * lax.rsqrt(ms + EPS) * fg_ref[...]
    o_ref[...] = y


def _attn_ffn(h, mods, norm_g, w_in, w_out, w_o, o_lat, o_ctx, layer, j, final_g=None):
    with_ctx = o_ctx is not None
    n_tok = T_ALL if with_ctx else T_LAT
    lat_tiles = T_LAT // TM
    row = lambda i: (i, 0)
    in_specs = [
        pl.BlockSpec((TM, D), row),
        pl.BlockSpec((None, None, N_MOD, D), lambda i: (layer, _mod_row(TM)(i), 0, 0)),
        pl.BlockSpec((None, None, 1, D), lambda i: (layer, 1, 0, 0)),
        _layer_resident((D, 2 * FH), layer),
        _layer_resident((FH, D), layer),
        _layer_resident((D, D), j),
        pl.BlockSpec((TM, D), lambda i: (jnp.minimum(i, lat_tiles - 1), 0)),
    ]
    args = [h, mods, norm_g, w_in, w_out, w_o, o_lat]
    if with_ctx:
        in_specs.append(pl.BlockSpec((TM, D), lambda i: (jnp.maximum(i - lat_tiles, 0), 0)))
        args.append(o_ctx)
    if final_g is not None:
        in_specs.append(_resident((1, D)))
        args.append(final_g)
    return pl.pallas_call(
        functools.partial(_attn_ffn_kernel, with_ctx=with_ctx, final_out=final_g is not None),
        grid=(n_tok // TM,),
        in_specs=in_specs,
        out_specs=pl.BlockSpec((TM, D), row),
        out_shape=jax.ShapeDtypeStruct((n_tok, D), f32),
        scratch_shapes=[pltpu.VMEM((TM, D), bf16), pltpu.VMEM((TM, FH), bf16)],
        compiler_params=_params("arbitrary"),
        name=f"attn_ffn{layer}",
    )(*args)


PW1_TN = 256


def _pw1_kernel(mod_ref, g_ref, w_ref, b_ref, *rest, n_h):
    o_ref, xn_scr = rest[n_h:]
    h = _load_tokens(rest[:n_h], TMP)
    xn_scr[...] = _norm_mod(h, g_ref[...], mod_ref[0:1, :], mod_ref[1:2, :]).astype(bf16)
    for c in range(D // PW1_TN):
        lo, hi = c * PW1_TN, (c + 1) * PW1_TN
        a = _dot(xn_scr[...], w_ref[:, lo:hi]) + b_ref[:, lo:hi]
        g = _dot(xn_scr[...], w_ref[:, D + lo:D + hi]) + b_ref[:, D + lo:D + hi]
        u = a * jax.nn.sigmoid(g)
        for cg in range(lo // LANES, hi // LANES):
            o_ref[pl.ds(cg, TMP, stride=CG), :] = u[:, cg * LANES - lo:(cg + 1) * LANES - lo]


def _pw1(h, mods, norm_g, w, b, layer, j, n_tok):
    h_specs, h_args = _token_specs(h, TMP)
    return pl.pallas_call(
        functools.partial(_pw1_kernel, n_h=len(h_args)),
        grid=(n_tok // TMP,),
        in_specs=[
            pl.BlockSpec((None, None, N_MOD, D), lambda i: (layer, _mod_row(TMP)(i), 0, 0)),
            pl.BlockSpec((None, None, 1, D), lambda i: (layer, 0, 0, 0)),
            _layer_resident((D, 2 * D), j),
            _resident((1, 2 * D)),
        ] + h_specs,
        out_specs=pl.BlockSpec((TMP * CG, LANES), lambda i: (i, 0)),
        out_shape=jax.ShapeDtypeStruct((n_tok * CG, LANES), f32),
        scratch_shapes=[pltpu.VMEM((TMP, D), bf16)],
        compiler_params=_params("arbitrary"),
        name=f"conv_pw1_{layer}",
    )(mods, norm_g, w, b, *h_args)


CONV_TB = 8


NSEG = TM // SEG
PAD_STEPS = SEG + 2 * HALO


def _fill_conv_pads(pad_scr, u_ref, left, right, is_ctx):
    for sg in range(NSEG):
        lo = sg * SEG
        before = left if sg == 0 else jnp.where(is_ctx, 0.0, u_ref[lo - HALO:lo])
        after = right if sg == NSEG - 1 else jnp.where(is_ctx, 0.0, u_ref[lo + SEG:lo + SEG + HALO])
        pad_scr[sg, 0:HALO] = before
        pad_scr[sg, HALO:HALO + SEG] = u_ref[lo:lo + SEG]
        pad_scr[sg, HALO + SEG:PAD_STEPS] = after


def _dwconv_block(pad_scr, sg, t0, dww_ref, cv_scr, aligned, after=None):
    taps = {}

    def tap(k):
        if k not in taps:
            taps[k] = dww_ref[k]
            if k == 0 and after is not None:
                always = jnp.logical_or(after == after, after != after)
                taps[k] = jnp.where(always, taps[k], 0.0)
        return taps[k]

    base = HALO - CONV_HALF
    acc = [None] * CONV_TB
    for j in range(CONV_TB + CONV_W - 1):
        x = pad_scr[sg, t0 + base + j]
        for i in range(CONV_TB):
            k = j - i
            if 0 <= k < CONV_W:
                term = tap(k) * x
                acc[i] = term if acc[i] is None else acc[i] + term
    for i in range(CONV_TB):
        cv_scr[pl.ds(aligned((sg * SEG + t0 + i) * CG), CG), :] = acc[i]
    return acc[CONV_TB - 1]


def _conv_ffn_kernel(u0_ref, u0n_ref, u_ref, up_ref, un_ref, mod_ref, g_ref, dww_ref, dwb_ref,
                     lng_ref, lnb_ref, w2_ref, b2_ref, win_ref, wout_ref, *rest, n_tiles, n_h):
    h_refs = rest[:n_h]
    o_ref, pad_scr, cv_scr, xn_scr, a_scr = rest[n_h:]
    s = pl.program_id(0)
    tiles_per_seq = L // TM

    @pl.when(s == 0)
    def _():
        _fill_conv_pads(pad_scr, u0_ref, jnp.zeros((HALO, CG, LANES), f32), u0n_ref[...], False)
        for sg in range(NSEG):
            def body(blk, carry, sg=sg):
                t0 = pl.multiple_of(blk * CONV_TB, CONV_TB)
                _dwconv_block(pad_scr, sg, t0, dww_ref, cv_scr, lambda r: pl.multiple_of(r, CG))
                return carry
            lax.fori_loop(0, SEG // CONV_TB, body, 0)

    v = jnp.concatenate([cv_scr[pl.ds(g, TM, stride=CG), :] for g in range(CG)], axis=1)
    v = v + dwb_ref[...]
    mu = jnp.mean(v, axis=-1, keepdims=True)
    vc = v - mu
    var = jnp.mean(vc * vc, axis=-1, keepdims=True)
    y = vc * lax.rsqrt(var + EPS) * lng_ref[...] + lnb_ref[...]
    y = (y * jax.nn.sigmoid(y)).astype(bf16)
    x = _load_tokens(h_refs, TM) + mod_ref[2:3, :] * (_dot(y, w2_ref[...]) + b2_ref[...])

    t = jnp.minimum(s + 1, n_tiles - 1)
    is_ctx = t >= T_LAT // TM
    first = jnp.logical_or(is_ctx, t % tiles_per_seq == 0)
    last = jnp.logical_or(is_ctx, t % tiles_per_seq == tiles_per_seq - 1)
    _fill_conv_pads(pad_scr, u_ref, jnp.where(first, 0.0, up_ref[...]), jnp.where(last, 0.0, un_ref[...]), is_ctx)
    next_conv = [functools.partial(_dwconv_block, pad_scr, sg, blk * CONV_TB, dww_ref, cv_scr, lambda r: r)
                 for sg in range(NSEG) for blk in range(SEG // CONV_TB)]
    o_ref[...] = _swiglu_residual(x, mod_ref, g_ref, win_ref, wout_ref, xn_scr, a_scr, side_work=next_conv)


def _conv_ffn(u, h, mods, norm_g, dw_w, dw_b, ln_g, ln_b, w2, b2, w_in, w_out, layer, j):
    n_tiles = T_ALL // TM
    nblk = T_ALL // HALO
    per = TM // HALO
    nxt = lambda s: jnp.minimum(s + 1, n_tiles - 1)
    u3 = u.reshape(T_ALL, CG, LANES)
    tile = (TM, CG, LANES)
    halo = (HALO, CG, LANES)
    h_specs, h_args = _token_specs(h, TM)
    return pl.pallas_call(
        functools.partial(_conv_ffn_kernel, n_tiles=n_tiles, n_h=len(h_args)),
        grid=(n_tiles,),
        in_specs=[
            pl.BlockSpec(tile, lambda s: (0, 0, 0), pipeline_mode=pl.Buffered(1)),
            pl.BlockSpec(halo, lambda s: (per, 0, 0), pipeline_mode=pl.Buffered(1)),
            pl.BlockSpec(tile, lambda s: (nxt(s), 0, 0)),
            pl.BlockSpec(halo, lambda s: (jnp.maximum(nxt(s) * per - 1, 0), 0, 0)),
            pl.BlockSpec(halo, lambda s: (jnp.minimum((nxt(s) + 1) * per, nblk - 1), 0, 0)),
            pl.BlockSpec((None, None, N_MOD, D), lambda s: (layer, _mod_row(TM)(s), 0, 0)),
            pl.BlockSpec((None, None, 1, D), lambda s: (layer, 1, 0, 0)),
            _resident((CONV_W, CG, LANES)),
            _resident((1, D)),
            _resident((1, D)),
            _resident((1, D)),
            _layer_resident((D, D), j),
            _resident((1, D)),
            _layer_resident((D, 2 * FH), layer),
            _layer_resident((FH, D), layer),
        ] + h_specs,
        out_specs=pl.BlockSpec((TM, D), lambda s: (s, 0)),
        out_shape=jax.ShapeDtypeStruct((T_ALL, D), f32),
        scratch_shapes=[pltpu.VMEM((NSEG, PAD_STEPS, CG, LANES), f32), pltpu.VMEM((TM * CG, LANES), f32),
                        pltpu.VMEM((TM, D), bf16), pltpu.VMEM((TM, FH), bf16)],
        compiler_params=_params("arbitrary"),
        name=f"conv_ffn{layer}",
    )(u3, u3, u3, u3, u3, mods, norm_g, dw_w.reshape(CONV_W, CG, LANES), dw_b, ln_g, ln_b, w2, b2,
      w_in, w_out, *h_args)


def _rope(x, cos, sin_signed, first_half):
    nxt = pltpu.roll(x, LANES - ROT // 2, axis=1)
    prv = pltpu.roll(x, ROT // 2, axis=1)
    return x * cos + jnp.where(first_half, nxt, prv) * sin_signed


def _qkv_kernel(h_ref, mod_ref, g_ref, w_ref, cos_ref, sin_ref, q_ref, k_ref, v_ref, xn_scr):
    xn_scr[...] = _norm_mod(h_ref[...], g_ref[...], mod_ref[0:1, :], mod_ref[1:2, :]).astype(bf16)
    cos = cos_ref[...]
    sin = sin_ref[...]
    lane = lax.broadcasted_iota(jnp.int32, (TMP, LANES), 1)
    first_half = (lane % ROT) < (ROT // 2)
    q = _dot(xn_scr[...], w_ref[:, 0:D])
    for hd in range(H):
        lanes = slice(hd * DV, (hd + 1) * DV)
        q_ref[:, lanes] = (_rope(q[:, lanes], cos, sin, first_half) * Q_SCALE).astype(bf16)
    k = _dot(xn_scr[...], w_ref[:, D:2 * D])
    for hd in range(H):
        lanes = slice(hd * DV, (hd + 1) * DV)
        k_ref[:, lanes] = _rope(k[:, lanes], cos, sin, first_half).astype(bf16)
    v_ref[...] = _dot(xn_scr[...], w_ref[:, 2 * D:3 * D]).astype(bf16)


def _qkv(h, mods, norm_g, w, cos_tab, sin_tab, layer, j):
    lat_tiles = T_LAT // TMP
    per_seq = L // TMP
    tab_idx = lambda i: (jnp.where(i < lat_tiles, i % per_seq, per_seq), 0)
    out = jax.ShapeDtypeStruct((T_ALL, D), bf16)
    return pl.pallas_call(
        _qkv_kernel,
        grid=(T_ALL // TMP,),
        in_specs=[
            pl.BlockSpec((TMP, D), lambda i: (i, 0)),
            pl.BlockSpec((None, None, N_MOD, D), lambda i: (layer, _mod_row(TMP)(i), 0, 0)),
            pl.BlockSpec((None, None, 1, D), lambda i: (layer, 0, 0, 0)),
            _layer_resident((D, 3 * D), j),
            pl.BlockSpec((TMP, LANES), tab_idx),
            pl.BlockSpec((TMP, LANES), tab_idx),
        ],
        out_specs=[pl.BlockSpec((TMP, D), lambda i: (i, 0))] * 3,
        out_shape=[out, out, out],
        scratch_shapes=[pltpu.VMEM((TMP, D), bf16)],
        compiler_params=_params("arbitrary"),
        name=f"attn_qkv_{layer}",
    )(h, mods, norm_g, w, cos_tab, sin_tab)


def _rope_tables():
    t = np.arange(L, dtype=np.float32)
    row = np.floor(t / GRID_W)
    col = t - row * GRID_W
    inv = (ROPE_BASE ** (-2.0 * np.arange(ROT // 2, dtype=np.float32) / ROT)).astype(np.float32)
    ang_r = row[:, None] * inv[None, :]
    ang_c = col[:, None] * inv[None, :]
    ang = np.concatenate([ang_r, ang_r, ang_c, ang_c], axis=-1)
    sign = np.where((np.arange(DH) % ROT) < ROT // 2, -1.0, 1.0).astype(np.float32)
    cos = np.cos(ang).astype(np.float32)
    sin = np.sin(ang).astype(np.float32) * sign[None, :]
    cos = np.concatenate([np.tile(cos, (1, 2)), np.ones((TMP, LANES), np.float32)], axis=0)
    sin = np.concatenate([np.tile(sin, (1, 2)), np.zeros((TMP, LANES), np.float32)], axis=0)
    return jnp.asarray(cos), jnp.asarray(sin)


def _attn_body(q_ref, k_refs, v_refs, lam_ref, g_ref, o_ref, lam_init, tq, tq_sub, lanes):
    lv = lam_ref[...]
    lam = (jnp.exp(jnp.sum(lv[0:1, :] * lv[1:2, :], axis=-1, keepdims=True))
           - jnp.exp(jnp.sum(lv[2:3, :] * lv[3:4, :], axis=-1, keepdims=True)) + lam_init)
    lane = lax.broadcasted_iota(jnp.int32, (tq_sub, LANES), 1)

    def scores(j):
        q = q_ref[j * tq_sub:(j + 1) * tq_sub, lanes]
        zero = jnp.zeros_like(q)
        qq = jnp.concatenate([jnp.where(lane < DH, q, zero), jnp.where(lane >= DH, q, zero)], axis=0)
        return [_dot_nt(qq, k_ref[:, lanes]) for k_ref in k_refs]

    vext = [jnp.concatenate([v_ref[:, lanes], jnp.ones((v_ref.shape[0], DV), bf16)], axis=1) for v_ref in v_refs]

    def finish(s):
        m = functools.reduce(jnp.maximum, [jnp.max(x, axis=-1, keepdims=True) for x in s])
        r = None
        for x, ve in zip(s, vext):
            part = _dot(jnp.exp2(x - m).astype(bf16), ve)
            r = part if r is None else r + part
        on = r[:, :DV] / r[:, DV:]
        o = on[:tq_sub] - lam * on[tq_sub:]
        ms = jnp.mean(o * o, axis=-1, keepdims=True)
        o = o * lax.rsqrt(ms + EPS) * g_ref[...] * (1.0 - lam_init)
        return o.astype(bf16)

    n = tq // tq_sub
    s = scores(0)
    for j in range(n):
        s_next = scores(j + 1) if j + 1 < n else None
        o_ref[j * tq_sub:(j + 1) * tq_sub, lanes] = finish(s)
        s = s_next


def _attn_kernel(*refs, lam_init, with_ctx):
    if with_ctx:
        ql_ref, qc_ref, kc_ref, kl_ref, vc_ref, vl_ref, lam_ref, g_ref, ol_ref, oc_ref = refs
    else:
        ql_ref, kc_ref, kl_ref, vc_ref, vl_ref, lam_ref, g_ref, ol_ref = refs
    for hh in range(HP):
        lanes = slice(hh * DV, (hh + 1) * DV)
        _attn_body(ql_ref, (kc_ref, kl_ref), (vc_ref, vl_ref), lam_ref, g_ref, ol_ref, lam_init, L, TQ_SUB, lanes)
        if with_ctx:
            _attn_body(qc_ref, (kc_ref,), (vc_ref,), lam_ref, g_ref, oc_ref, lam_init, LC, LC, lanes)


def _attention(q, k, v, lam_vec, subln_g, lam_init, with_ctx):
    ctx0 = T_LAT // LC
    lat_spec = pl.BlockSpec((L, HP * DV), lambda b, h: (b, h))
    ctx_spec = pl.BlockSpec((LC, HP * DV), lambda b, h: (ctx0 + b, h))
    small = [pl.BlockSpec((4, DH), lambda b, h: (0, 0)), pl.BlockSpec((1, DV), lambda b, h: (0, 0))]
    q_specs, q_args = ([lat_spec, ctx_spec], [q, q]) if with_ctx else ([lat_spec], [q])
    out_specs = [lat_spec]
    out_shape = [jax.ShapeDtypeStruct((T_LAT, D), bf16)]
    if with_ctx:
        out_specs.append(pl.BlockSpec((LC, HP * DV), lambda b, h: (b, h)))
        out_shape.append(jax.ShapeDtypeStruct((T_CTX, D), bf16))
    return pl.pallas_call(
        functools.partial(_attn_kernel, lam_init=lam_init, with_ctx=with_ctx),
        grid=(B, H // HP),
        in_specs=q_specs + [ctx_spec, lat_spec, ctx_spec, lat_spec] + small,
        out_specs=out_specs,
        out_shape=out_shape,
        compiler_params=_params("arbitrary", "arbitrary"),
        name="diff_attn",
    )(*q_args, k, k, v, v, lam_vec, subln_g)


def kernel(x, c, ctx, c_ctx, mod_w, mod_b, norm_g, conv_pw1_w, conv_pw1_b, conv_dw_w, conv_dw_b, conv_ln_g, conv_ln_b, conv_pw2_w, conv_pw2_b, attn_w_qkv, attn_lambda, attn_subln_g, attn_w_o, ffn_w_in, ffn_w_out, final_g):
    assert x.shape == (B, L, D) and ctx.shape == (B, LC, D)
    h = (x.reshape(T_LAT, D), ctx.reshape(T_CTX, D))
    cvec = jnp.concatenate([c, c_ctx[None, :], jnp.zeros((MOD_ROWS - B - 1, D), f32)], axis=0)
    mods = _modulation(cvec, mod_w, mod_b).reshape(DEPTH, MOD_ROWS, N_MOD, D)
    norm_g4 = norm_g.reshape(DEPTH, 2, 1, D)
    cos_tab, sin_tab = _rope_tables()
    w_in, w_out = ffn_w_in.astype(bf16), ffn_w_out.astype(bf16)
    w_pw1, w_pw2 = conv_pw1_w.astype(bf16), conv_pw2_w.astype(bf16)
    w_qkv, w_o = attn_w_qkv.astype(bf16), attn_w_o.astype(bf16)

    for i in range(DEPTH):
        with_ctx = i < DEPTH - 1
        j = i // N_MIXERS
        if i % N_MIXERS == 0:
            assert with_ctx
            u = _pw1(h, mods, norm_g4, w_pw1, conv_pw1_b[j].reshape(1, 2 * D), i, j, T_ALL)
            h = _conv_ffn(u, h, mods, norm_g4, conv_dw_w[j], conv_dw_b[j].reshape(1, D),
                          conv_ln_g[j].reshape(1, D), conv_ln_b[j].reshape(1, D), w_pw2,
                          conv_pw2_b[j].reshape(1, D), w_in, w_out, i, j)
        else:
            lam_init = 0.8 - 0.6 * math.exp(-0.3 * i)
            q, k, v = _qkv(h, mods, norm_g4, w_qkv, cos_tab, sin_tab, i, j)
            o = _attention(q, k, v, attn_lambda[j], attn_subln_g[j].reshape(1, DV), lam_init, with_ctx)
            o_lat, o_ctx = o if with_ctx else (o[0], None)
            h = _attn_ffn(h, mods, norm_g4, w_in, w_out, w_o, o_lat, o_ctx, i, j,
                          final_g=final_g.reshape(1, D) if i == DEPTH - 1 else None)
    return h.reshape(B, L, D)
```
